```python
import math
import jax, jax.numpy as jnp
from jax import lax
import numpy as np

D_MODEL = 1024
BATCH = 8
SEQ = 2048
DEPTH = 4
DEC_BATCH = 16
DEC_SEQ = 2048
PAST_LEN = 128

N_MIXERS = 2
N_ATTN_LAYERS = (DEPTH + 1) // 2
N_CONV_LAYERS = DEPTH // 2
ATTN_HEAD_DIM = 128
ATTN_HEADS = D_MODEL // ATTN_HEAD_DIM
WINDOWS = (128, 512, 2048)
DILATIONS = (1, 4, 16)
N_GROUPS = len(WINDOWS)
QKV_COLS = N_GROUPS * 3 * ATTN_HEADS * ATTN_HEAD_DIM
ROPE_THETA = 500000.0
ROT_DIM = ATTN_HEAD_DIM // 4
NEG_INF = -1e30
CONV_WIDTH = 3
PEER_HEADS = 8
PEER_N_KEYS = 128
PEER_N_EXPERTS = PEER_N_KEYS * PEER_N_KEYS
PEER_QUERY_DIM = 256
PEER_HALF = PEER_QUERY_DIM // 2
PEER_TOPK = 16
PEER_CHUNK = 128
PLE_DIM = 256
DEEPNORM_ALPHA = (2.0 * DEPTH) ** 0.25
DEEPNORM_BETA = (8.0 * DEPTH) ** -0.25
LN_EPS = 1e-5

kernel_name = "hybrid_dilated_attn_shortconv_peer_encoder"


def layer_norm(x, g, b):
    xf = x.astype(jnp.float32)
    mu = jnp.mean(xf, axis=-1, keepdims=True)
    xc = xf - mu
    var = jnp.mean(xc * xc, axis=-1, keepdims=True)
    return (xc * lax.rsqrt(var + LN_EPS) * g.astype(jnp.float32) + b.astype(jnp.float32)).astype(x.dtype)


def rope_tables(seq_len):
    inv_freq = ROPE_THETA ** (-jnp.arange(0, ROT_DIM, 2, dtype=jnp.float32) / ROT_DIM)
    ang = jnp.arange(seq_len, dtype=jnp.float32)[:, None] * inv_freq[None, :]
    return jnp.cos(ang), jnp.sin(ang)


def apply_partial_rope(t, cos, sin):
    rot = t[..., :ROT_DIM].astype(jnp.float32)
    x1, x2 = rot[..., :ROT_DIM // 2], rot[..., ROT_DIM // 2:]
    c, s = cos[None, :, None, :], sin[None, :, None, :]
    rotated = jnp.concatenate([x1 * c - x2 * s, x2 * c + x1 * s], axis=-1).astype(t.dtype)
    return jnp.concatenate([rotated, t[..., ROT_DIM:]], axis=-1)


def dilated_window_attention(q, k, v, dil, steps):
    B, S, H, Dh = q.shape
    L = S // dil
    Lp = -(-L // steps) * steps
    nb = Lp // steps
    pad = Lp - L
    Bd = B * dil

    def to_res(t):
        return t.reshape(B, L, dil, H, Dh).transpose(0, 2, 1, 3, 4).reshape(Bd, L, H, Dh)

    qb = jnp.pad(to_res(q), ((0, 0), (0, pad), (0, 0), (0, 0))).reshape(Bd, nb, steps, H, Dh)

    def windows(t):
        tp = jnp.pad(to_res(t), ((0, 0), (steps, steps + pad), (0, 0), (0, 0)))
        tp = tp.reshape(Bd, nb + 2, steps, H, Dh)
        return jnp.concatenate([tp[:, :-2], tp[:, 1:-1], tp[:, 2:]], axis=2)

    kw, vw = windows(k), windows(v)
    qpos = jnp.arange(Lp).reshape(nb, steps)
    kpos = jnp.arange(nb)[:, None] * steps - steps + jnp.arange(3 * steps)[None, :]
    dq = qpos[:, :, None] - kpos[:, None, :]
    valid = (jnp.abs(dq) <= steps) & (kpos[:, None, :] >= 0) & (kpos[:, None, :] < L)

    scale = 1.0 / math.sqrt(Dh)
    s = jnp.einsum('xnqhd,xnkhd->xnhqk', qb, kw).astype(jnp.float32) * scale
    s = jnp.where(valid[None, :, None], s, NEG_INF)
    m = jnp.max(s, axis=-1, keepdims=True)
    p = jnp.exp(s - m)
    den = jnp.sum(p, axis=-1)
    o = jnp.einsum('xnhqk,xnkhd->xnqhd', p, vw.astype(jnp.float32))
    o = o / jnp.transpose(den, (0, 1, 3, 2))[..., None]
    lse = m[..., 0] + jnp.log(den)

    o = o.reshape(Bd, Lp, H, Dh)[:, :L]
    o = o.reshape(B, dil, L, H, Dh).transpose(0, 2, 1, 3, 4).reshape(B, S, H, Dh)
    lse = jnp.transpose(lse, (0, 1, 3, 2)).reshape(Bd, Lp, H)[:, :L]
    lse = lse.reshape(B, dil, L, H).transpose(0, 2, 1, 3).reshape(B, S, H)
    return o, lse


def dilated_mixture_attention(x, w_qkv, w_out):
    B, S, _ = x.shape
    qkv = (x @ w_qkv).reshape(B, S, N_GROUPS, 3, ATTN_HEADS, ATTN_HEAD_DIM)
    cos, sin = rope_tables(S)
    outs, lses = [], []
    for g in range(N_GROUPS):
        dil = DILATIONS[g]
        steps = WINDOWS[g] // (2 * dil)
        q = apply_partial_rope(qkv[:, :, g, 0], cos, sin)
        k = apply_partial_rope(qkv[:, :, g, 1], cos, sin)
        o, lse = dilated_window_attention(q, k, qkv[:, :, g, 2], dil, steps)
        outs.append(o)
        lses.append(lse)
    wgt = jax.nn.softmax(jnp.stack(lses, axis=0), axis=0)
    o = jnp.sum(wgt[..., None] * jnp.stack(outs, axis=0), axis=0).astype(x.dtype)
    return o.reshape(B, S, ATTN_HEADS * ATTN_HEAD_DIM) @ w_out


def short_conv_mixer(x, w_in, conv_kernel, w_out):
    gb, gc, xv = jnp.split(x @ w_in, 3, axis=-1)
    h = gc * xv
    hp = jnp.pad(h, ((0, 0), (1, 1), (0, 0)))
    conv = hp[:, :-2] * conv_kernel[0] + hp[:, 1:-1] * conv_kernel[1] + hp[:, 2:] * conv_kernel[2]
    return (gb * conv) @ w_out


def peer_channel_mixer(x, w_query, sub_keys, u_tab, v_tab):
    B, S, D = x.shape
    xt = x.reshape(-1, PEER_CHUNK, D)

    def block(xc):
        C = xc.shape[0]
        q = (xc @ w_query).reshape(C, PEER_HEADS, 2, PEER_HALF)
        s = jnp.einsum('chpk,hpnk->chpn', q, sub_keys).astype(jnp.float32)
        sv, si = lax.top_k(s, PEER_TOPK)
        cand = (sv[:, :, 0, :, None] + sv[:, :, 1, None, :]).reshape(C, PEER_HEADS, PEER_TOPK * PEER_TOPK)
        cidx = (si[:, :, 0, :, None] * PEER_N_KEYS + si[:, :, 1, None, :]).reshape(C, PEER_HEADS, PEER_TOPK * PEER_TOPK)
        fv, fi = lax.top_k(cand, PEER_TOPK)
        eidx = jnp.take_along_axis(cidx, fi, axis=-1)
        g = jax.nn.softmax(fv, axis=-1)
        u = u_tab[eidx]
        h = jax.nn.gelu(jnp.einsum('chkd,cd->chk', u, xc).astype(jnp.float32), approximate=False)
        coef = (g * h).astype(xc.dtype)
        return jnp.einsum('chk,chkd->cd', coef, v_tab[eidx])

    return lax.map(block, xt).reshape(B, S, D)


def trunk(x, p, attn_w_qkv, attn_w_out, conv_w_in, conv_kernel, conv_w_out,
          peer_w_query, peer_sub_keys, peer_u, peer_v,
          ln1_gain, ln1_bias, ln2_gain, ln2_bias, ple_w_proj, ple_w_gate):
    for i in range(DEPTH):
        j = i // N_MIXERS
        if i % N_MIXERS == 0:
            h = dilated_mixture_attention(x, attn_w_qkv[j], attn_w_out[j])
        else:
            h = short_conv_mixer(x, conv_w_in[j], conv_kernel[j], conv_w_out[j])
        x = layer_norm(DEEPNORM_ALPHA * x + h, ln1_gain[i], ln1_bias[i])
        f = peer_channel_mixer(x, peer_w_query[i], peer_sub_keys[i], peer_u[i], peer_v[i])
        x = layer_norm(DEEPNORM_ALPHA * x + f, ln2_gain[i], ln2_bias[i])
        x = x + jax.nn.sigmoid(x @ ple_w_gate[i]) * (p[i] @ ple_w_proj[i])
    return x


def setup_inputs(seed: int = 0) -> dict:
    key = jax.random.key(seed)
    ks = jax.random.split(key, 20)
    f32 = jnp.float32
    D = D_MODEL
    nrm = lambda k, shape, scale: jax.random.normal(k, shape, f32) * scale
    return {
        "x_prompt": nrm(ks[0], (BATCH, SEQ, D), 1.0),
        "x_sample": nrm(ks[1], (DEC_BATCH, DEC_SEQ, D), 1.0),
        "p_prompt": nrm(ks[2], (DEPTH, BATCH, SEQ, PLE_DIM), 1.0),
        "p_sample": nrm(ks[3], (DEPTH, DEC_BATCH, DEC_SEQ, PLE_DIM), 1.0),
        "attn_w_qkv": nrm(ks[4], (N_ATTN_LAYERS, D, QKV_COLS), D ** -0.5),
        "attn_w_out": nrm(ks[5], (N_ATTN_LAYERS, ATTN_HEADS * ATTN_HEAD_DIM, D), DEEPNORM_BETA * (ATTN_HEADS * ATTN_HEAD_DIM) ** -0.5),
        "conv_w_in": nrm(ks[6], (N_CONV_LAYERS, D, 3 * D), D ** -0.5),
        "conv_kernel": nrm(ks[7], (N_CONV_LAYERS, CONV_WIDTH, D), CONV_WIDTH ** -0.5),
        "conv_w_out": nrm(ks[8], (N_CONV_LAYERS, D, D), DEEPNORM_BETA * D ** -0.5),
        "peer_w_query": nrm(ks[9], (DEPTH, D, PEER_HEADS * PEER_QUERY_DIM), D ** -0.5),
        "peer_sub_keys": nrm(ks[10], (DEPTH, PEER_HEADS, 2, PEER_N_KEYS, PEER_HALF), PEER_HALF ** -0.5),
        "peer_u": nrm(ks[11], (DEPTH, PEER_N_EXPERTS, D), D ** -0.5),
        "peer_v": nrm(ks[12], (DEPTH, PEER_N_EXPERTS, D), DEEPNORM_BETA * PEER_HEADS ** -0.5),
        "ln1_gain": 1.0 + nrm(ks[13], (DEPTH, D), 0.01),
        "ln1_bias": nrm(ks[14], (DEPTH, D), 0.01),
        "ln2_gain": 1.0 + nrm(ks[15], (DEPTH, D), 0.01),
        "ln2_bias": nrm(ks[16], (DEPTH, D), 0.01),
        "ple_w_proj": nrm(ks[17], (DEPTH, PLE_DIM, D), PLE_DIM ** -0.5),
        "ple_w_gate": nrm(ks[18], (DEPTH, D, D), D ** -0.5),
    }


def reference(x_prompt, x_sample, p_prompt, p_sample, attn_w_qkv, attn_w_out, conv_w_in,
              conv_kernel, conv_w_out, peer_w_query, peer_sub_keys, peer_u, peer_v,
              ln1_gain, ln1_bias, ln2_gain, ln2_bias, ple_w_proj, ple_w_gate):
    y_prompt = trunk(x_prompt, p_prompt, attn_w_qkv, attn_w_out, conv_w_in, conv_kernel, conv_w_out,
                     peer_w_query, peer_sub_keys, peer_u, peer_v,
                     ln1_gain, ln1_bias, ln2_gain, ln2_bias, ple_w_proj, ple_w_gate)
    y_sample = trunk(x_sample, p_sample, attn_w_qkv, attn_w_out, conv_w_in, conv_kernel, conv_w_out,
                     peer_w_query, peer_sub_keys, peer_u, peer_v,
                     ln1_gain, ln1_bias, ln2_gain, ln2_bias, ple_w_proj, ple_w_gate)
    return (y_prompt, y_sample)
```

```python
import functools
import math

import jax
import jax.numpy as jnp
from jax import lax
from jax.experimental import pallas as pl
from jax.experimental.pallas import tpu as pltpu

F32 = jnp.float32
BF16 = jnp.bfloat16

D_MODEL = 1024
SEQ = 2048
DEPTH = 4
N_MIXERS = 2
HEAD_DIM = 128
N_HEADS = D_MODEL // HEAD_DIM
WINDOWS = (128, 512, 2048)
DILATIONS = (1, 4, 16)
N_GROUPS = 3
GROUP_COLS = 3 * N_HEADS * HEAD_DIM
ROPE_THETA = 500000.0
ROT_DIM = HEAD_DIM // 4
NEG_INF = -1e30
PEER_HEADS = 8
PEER_KEYS = 128
PEER_EXPERTS = PEER_KEYS * PEER_KEYS
PEER_HALF = 128
PEER_TOPK = 16
PLE_DIM = 256
ALPHA = (2.0 * DEPTH) ** 0.25
LN_EPS = 1e-5
INV_SQRT2 = 0.7071067811865476

VMEM_LIMIT_BYTES = 56 * 1024 * 1024
LANES = 128

TM = 512
QB = 128
HALF_W = 64
TS = 256
TT = 512
ET = 1024
HALO = 8


def _cparams(sem):
    return pltpu.CompilerParams(dimension_semantics=sem, vmem_limit_bytes=VMEM_LIMIT_BYTES)


def _layer_norm(z, g, b):
    mu = jnp.mean(z, axis=-1, keepdims=True)
    zc = z - mu
    var = jnp.mean(zc * zc, axis=-1, keepdims=True)
    return zc * lax.rsqrt(var + LN_EPS) * g + b


def _qkv_kernel(x_ref, w_ref, o_ref):
    xb = x_ref[...].astype(BF16)
    o_ref[...] = jnp.dot(xb, w_ref[...], preferred_element_type=F32).astype(o_ref.dtype)


def _qkv_proj(x2, w_qkv):
    t = x2.shape[0]
    n = w_qkv.shape[1]
    return pl.pallas_call(
        _qkv_kernel,
        grid=(n // GROUP_COLS, t // TM),
        in_specs=[
            pl.BlockSpec((TM, D_MODEL), lambda j, i: (i, 0)),
            pl.BlockSpec((D_MODEL, GROUP_COLS), lambda j, i: (0, j)),
        ],
        out_specs=pl.BlockSpec((TM, GROUP_COLS), lambda j, i: (i, j)),
        out_shape=jax.ShapeDtypeStruct((t, n), BF16),
        compiler_params=_cparams(("arbitrary", "arbitrary")),
        name="qkv_proj",
    )(x2, w_qkv)


def _attn_kernel(q0, k0, v0, q1, k1, v1, q2, k2, v2, c_ref, sa_ref, sb_ref, o_ref,
                 qr, kr, vf, m_scr, l_scr, acc_scr):
    scale = 1.0 / math.sqrt(HEAD_DIM)
    m_scr[...] = jnp.full(m_scr.shape, NEG_INF, F32)
    l_scr[...] = jnp.zeros(l_scr.shape, F32)
    acc_scr[...] = jnp.zeros(acc_scr.shape, F32)

    def rope(t):
        return (t * c_ref[...]
                + pltpu.roll(t, LANES - ROT_DIM // 2, axis=1) * sa_ref[...]
                + pltpu.roll(t, ROT_DIM // 2, axis=1) * sb_ref[...])

    for (q_ref, k_ref, v_ref), dil in zip(((q0, k0, v0), (q1, k1, v1), (q2, k2, v2)), DILATIONS):
        sub_len = SEQ // dil
        nb = sub_len // QB
        kw = min(QB + 2 * HALF_W, sub_len)
        qr[...] = rope(q_ref[...].astype(F32))
        kr[...] = rope(k_ref[...].astype(F32))
        vf[...] = v_ref[...].astype(F32)

        def block(bi, carry, dil=dil, sub_len=sub_len, nb=nb, kw=kw):
            r = bi // nb
            m0 = (bi % nb) * QB
            ws = jnp.clip(m0 - HALF_W, 0, sub_len - kw)
            if dil == 1:
                qrows = pl.ds(pl.multiple_of(m0, QB), QB)
                krows = pl.ds(pl.multiple_of(ws, HALF_W), kw)
            else:
                qrows = pl.ds(r + dil * m0, QB, stride=dil)
                krows = pl.ds(r + dil * ws, kw, stride=dil)
            q = qr[qrows, :].astype(BF16)
            k = kr[krows, :].astype(BF16)
            v = vf[krows, :].astype(BF16)
            s = lax.dot_general(q, k, (((1,), (1,)), ((), ())), preferred_element_type=F32) * scale
            qi = m0 + lax.broadcasted_iota(jnp.int32, (QB, kw), 0)
            ki = ws + lax.broadcasted_iota(jnp.int32, (QB, kw), 1)
            s = jnp.where(jnp.abs(qi - ki) <= HALF_W, s, NEG_INF)
            m_old = m_scr[qrows, :]
            m_new = jnp.maximum(m_old, jnp.max(s, axis=-1, keepdims=True))
            p = jnp.exp(s - m_new)
            alpha = jnp.exp(m_old - m_new)
            l_scr[qrows, :] = alpha * l_scr[qrows, :] + jnp.sum(p, axis=-1, keepdims=True)
            acc_scr[qrows, :] = alpha * acc_scr[qrows, :] + jnp.dot(
                p.astype(BF16), v, preferred_element_type=F32)
            m_scr[qrows, :] = m_new
            return carry

        lax.fori_loop(0, dil * nb, block, 0)

    o_ref[...] = (acc_scr[...] / l_scr[...]).astype(o_ref.dtype)


def _attention(qkv3, c_tab, sa_tab, sb_tab):
    nbatch = qkv3.shape[0]

    def col(g, part):
        return lambda b, h: (b, 0, g * 3 * N_HEADS + part * N_HEADS + h)

    in_specs = []
    for g in range(N_GROUPS):
        for part in range(3):
            in_specs.append(pl.BlockSpec((None, SEQ, HEAD_DIM), col(g, part)))
    tab_spec = pl.BlockSpec((SEQ, HEAD_DIM), lambda b, h: (0, 0))
    in_specs += [tab_spec, tab_spec, tab_spec]
    return pl.pallas_call(
        _attn_kernel,
        grid=(nbatch, N_HEADS),
        in_specs=in_specs,
        out_specs=pl.BlockSpec((None, SEQ, HEAD_DIM), lambda b, h: (b, 0, h)),
        out_shape=jax.ShapeDtypeStruct((nbatch, SEQ, D_MODEL), BF16),
        scratch_shapes=[
            pltpu.VMEM((SEQ, HEAD_DIM), F32),
            pltpu.VMEM((SEQ, HEAD_DIM), F32),
            pltpu.VMEM((SEQ, HEAD_DIM), F32),
            pltpu.VMEM((SEQ, 1), F32),
            pltpu.VMEM((SEQ, 1), F32),
            pltpu.VMEM((SEQ, HEAD_DIM), F32),
        ],
        compiler_params=_cparams(("arbitrary", "arbitrary")),
        name="dilated_attention",
    )(*([qkv3] * 9), c_tab, sa_tab, sb_tab)


def _proj_ln_kernel(x_ref, o_ref, w_ref, g_ref, b_ref, y_ref):
    h = jnp.dot(o_ref[...], w_ref[...], preferred_element_type=F32)
    y_ref[...] = _layer_norm(ALPHA * x_ref[...] + h, g_ref[...], b_ref[...])


def _proj_ln(x2, o2, w, g, b):
    t = x2.shape[0]
    row = pl.BlockSpec((TM, D_MODEL), lambda i: (i, 0))
    vec = pl.BlockSpec((1, D_MODEL), lambda i: (0, 0))
    return pl.pallas_call(
        _proj_ln_kernel,
        grid=(t // TM,),
        in_specs=[row, row, pl.BlockSpec((D_MODEL, D_MODEL), lambda i: (0, 0)), vec, vec],
        out_specs=row,
        out_shape=jax.ShapeDtypeStruct((t, D_MODEL), F32),
        compiler_params=_cparams(("arbitrary",)),
        name="attn_out_ln",
    )(x2, o2, w, g, b)


def _conv_kernel(xp_ref, x_ref, xn_ref, win_ref, ck_ref, wout_ref, g_ref, b_ref, y_ref, xcat, u_scr):
    i = pl.program_id(0)
    xcat[0:HALO, :] = xp_ref[0].astype(BF16)
    xcat[HALO:HALO + TM, :] = x_ref[...].astype(BF16)
    xcat[HALO + TM:, :] = xn_ref[0].astype(BF16)
    u_scr[...] = jnp.dot(xcat[...], win_ref[...], preferred_element_type=F32)
    hh = u_scr[:, D_MODEL:2 * D_MODEL] * u_scr[:, 2 * D_MODEL:]
    pos = (i * TM + lax.broadcasted_iota(jnp.int32, (TM, 1), 0)) % SEQ
    prev = jnp.where(pos == 0, 0.0, hh[HALO - 1:HALO - 1 + TM, :])
    nxt = jnp.where(pos == SEQ - 1, 0.0, hh[HALO + 1:HALO + 1 + TM, :])
    ck = ck_ref[...]
    conv = prev * ck[0:1, :] + hh[HALO:HALO + TM, :] * ck[1:2, :] + nxt * ck[2:3, :]
    gated = (u_scr[HALO:HALO + TM, 0:D_MODEL] * conv).astype(BF16)
    h = jnp.dot(gated, wout_ref[...], preferred_element_type=F32)
    y_ref[...] = _layer_norm(ALPHA * x_ref[...] + h, g_ref[...], b_ref[...])


def _conv_mixer_ln(x2, w_in, ck, w_out, g, b):
    t = x2.shape[0]
    nblk = t // HALO
    x3 = x2.reshape(nblk, HALO, D_MODEL)
    per = TM // HALO
    row = pl.BlockSpec((TM, D_MODEL), lambda i: (i, 0))
    vec = pl.BlockSpec((1, D_MODEL), lambda i: (0, 0))
    return pl.pallas_call(
        _conv_kernel,
        grid=(t // TM,),
        in_specs=[
            pl.BlockSpec((1, HALO, D_MODEL), lambda i: (jnp.maximum(i * per - 1, 0), 0, 0)),
            row,
            pl.BlockSpec((1, HALO, D_MODEL), lambda i: (jnp.minimum((i + 1) * per, nblk - 1), 0, 0)),
            pl.BlockSpec((D_MODEL, 3 * D_MODEL), lambda i: (0, 0)),
            pl.BlockSpec((3, D_MODEL), lambda i: (0, 0)),
            pl.BlockSpec((D_MODEL, D_MODEL), lambda i: (0, 0)),
            vec, vec,
        ],
        out_specs=row,
        out_shape=jax.ShapeDtypeStruct((t, D_MODEL), F32),
        scratch_shapes=[
            pltpu.VMEM((TM + 2 * HALO, D_MODEL), BF16),
            pltpu.VMEM((TM + 2 * HALO, 3 * D_MODEL), F32),
        ],
        compiler_params=_cparams(("arbitrary",)),
        name="conv_mixer_ln",
    )(x3, x2, x3, w_in, ck, w_out, g, b)


def _top16_desc(s):
    row = lax.broadcasted_iota(jnp.int32, (PEER_TOPK, s.shape[1]), 0)
    vals = jnp.zeros((PEER_TOPK, s.shape[1]), F32)
    work = s
    for a in range(PEER_TOPK):
        mx = jnp.max(work, axis=0, keepdims=True)
        vals = jnp.where(row == a, mx, vals)
        if a + 1 < PEER_TOPK:
            work = jnp.where(work == mx, -jnp.inf, work)
    return vals


def _select_block(s0, s1):
    lanes = s0.shape[1]
    v0 = _top16_desc(s0)
    v1 = _top16_desc(s1)
    cand = [v0[a:a + 1, :] + v1 for a in range(PEER_TOPK)]
    work = list(cand)
    tau = None
    for k in range(PEER_TOPK):
        m = work[0]
        for a in range(1, PEER_TOPK):
            m = jnp.maximum(m, work[a])
        tau = jnp.max(m, axis=0, keepdims=True)
        if k + 1 < PEER_TOPK:
            work = [jnp.where(w == tau, -jnp.inf, w) for w in work]
    top = v0[0:1, :] + v1[0:1, :]
    rowb = lax.broadcasted_iota(jnp.int32, (PEER_TOPK, lanes), 0).astype(F32)
    z = jnp.zeros((1, lanes), F32)
    betas = []
    for a in range(PEER_TOPK):
        sel = cand[a] >= tau
        z = z + jnp.sum(jnp.where(sel, jnp.exp(cand[a] - top), 0.0), axis=0, keepdims=True)
        cnt = jnp.sum(jnp.where(sel, 1.0, 0.0), axis=0, keepdims=True)
        beta = jnp.sum(jnp.where(rowb == cnt - 1.0, v1, 0.0), axis=0, keepdims=True)
        betas.append(jnp.where(cnt > 0.5, beta, jnp.inf))
    th = jnp.full(s0.shape, jnp.inf, F32)
    for a in range(PEER_TOPK):
        th = jnp.where(s0 == v0[a:a + 1, :], betas[a], th)
    a_fac = jnp.exp(s0 - v0[0:1, :]) * (1.0 / z)
    b_fac = jnp.exp(s1 - v1[0:1, :])
    return a_fac, th, b_fac


def _peer_select_kernel(x_ref, wq_ref, sk_ref, a_ref, th_ref, s1_ref, b_ref):
    xb = x_ref[...].astype(BF16)
    q = jnp.dot(xb, wq_ref[...], preferred_element_type=F32).astype(BF16)
    nt = (((1,), (1,)), ((), ()))
    s0 = lax.dot_general(sk_ref[0], q[:, :PEER_HALF], nt, preferred_element_type=F32)
    s1 = lax.dot_general(sk_ref[1], q[:, PEER_HALF:], nt, preferred_element_type=F32)
    for tb in range(TS // LANES):
        sl = slice(tb * LANES, (tb + 1) * LANES)
        a_fac, th, b_fac = _select_block(s0[:, sl], s1[:, sl])
        a_ref[:, sl] = a_fac
        th_ref[:, sl] = th
        s1_ref[:, sl] = s1[:, sl]
        b_ref[:, sl] = b_fac


def _peer_select(x2, wq, sk):
    t = x2.shape[0]
    out = jax.ShapeDtypeStruct((PEER_HEADS, PEER_KEYS, t), F32)
    ospec = pl.BlockSpec((None, PEER_KEYS, TS), lambda i, h: (h, 0, i))
    return pl.pallas_call(
        _peer_select_kernel,
        grid=(t // TS, PEER_HEADS),
        in_specs=[
            pl.BlockSpec((TS, D_MODEL), lambda i, h: (i, 0)),
            pl.BlockSpec((D_MODEL, 2 * PEER_HALF), lambda i, h: (0, h)),
            pl.BlockSpec((None, 2, PEER_KEYS, PEER_HALF), lambda i, h: (h, 0, 0, 0)),
        ],
        out_specs=[ospec, ospec, ospec, ospec],
        out_shape=[out, out, out, out],
        compiler_params=_cparams(("arbitrary", "arbitrary")),
        name="peer_select",
    )(x2, wq, sk)


def _peer_dense_kernel(x_ref, a_ref, th_ref, s1_ref, b_ref, u_ref, vt_ref, g_ref, bb_ref,
                       p_ref, wg_ref, wp_ref, y_ref, xt_scr, acc_scr, h_scr, c_scr):
    e = pl.program_id(1)
    rows_per_step = ET // PEER_KEYS

    @pl.when(e == 0)
    def _():
        xt_scr[...] = x_ref[...].T.astype(BF16)
        acc_scr[...] = jnp.zeros(acc_scr.shape, F32)

    h_scr[...] = jnp.dot(u_ref[...], xt_scr[...], preferred_element_type=F32)
    irows = pl.ds(pl.multiple_of(e * rows_per_step, rows_per_step), rows_per_step)
    for il in range(rows_per_step):
        for tb in range(TT // LANES):
            sl = slice(tb * LANES, (tb + 1) * LANES)
            w = jnp.zeros((PEER_KEYS, LANES), F32)
            for hd in range(PEER_HEADS):
                a_row = a_ref[hd, irows, sl][il:il + 1, :]
                th_row = th_ref[hd, irows, sl][il:il + 1, :]
                w = w + a_row * jnp.where(s1_ref[hd, :, sl] >= th_row, b_ref[hd, :, sl], 0.0)
            hv = h_scr[il * PEER_KEYS:(il + 1) * PEER_KEYS, sl]
            gelu = 0.5 * hv * (1.0 + lax.erf(hv * INV_SQRT2))
            c_scr[il * PEER_KEYS:(il + 1) * PEER_KEYS, sl] = (w * gelu).astype(BF16)
    acc_scr[...] += jnp.dot(vt_ref[...], c_scr[...], preferred_element_type=F32)

    @pl.when(e == pl.num_programs(1) - 1)
    def _():
        f = acc_scr[...].T
        y = _layer_norm(ALPHA * x_ref[...] + f, g_ref[...], bb_ref[...])
        gate = jax.nn.sigmoid(jnp.dot(y.astype(BF16), wg_ref[...], preferred_element_type=F32))
        proj = jnp.dot(p_ref[...].astype(BF16), wp_ref[...], preferred_element_type=F32)
        y_ref[...] = y + gate * proj


def _peer_dense(x2, sel, u, vt, g, b, p2, wg, wp):
    t = x2.shape[0]
    row = pl.BlockSpec((TT, D_MODEL), lambda i, e: (i, 0))
    vec = pl.BlockSpec((1, D_MODEL), lambda i, e: (0, 0))
    fac = pl.BlockSpec((PEER_HEADS, PEER_KEYS, TT), lambda i, e: (0, 0, i))
    return pl.pallas_call(
        _peer_dense_kernel,
        grid=(t // TT, PEER_EXPERTS // ET),
        in_specs=[
            row, fac, fac, fac, fac,
            pl.BlockSpec((ET, D_MODEL), lambda i, e: (e, 0)),
            pl.BlockSpec((D_MODEL, ET), lambda i, e: (0, e)),
            vec, vec,
            pl.BlockSpec((TT, PLE_DIM), lambda i, e: (i, 0)),
            pl.BlockSpec((D_MODEL, D_MODEL), lambda i, e: (0, 0)),
            pl.BlockSpec((PLE_DIM, D_MODEL), lambda i, e: (0, 0)),
        ],
        out_specs=row,
        out_shape=jax.ShapeDtypeStruct((t, D_MODEL), F32),
        scratch_shapes=[
            pltpu.VMEM((D_MODEL, TT), BF16),
            pltpu.VMEM((D_MODEL, TT), F32),
            pltpu.VMEM((ET, TT), F32),
            pltpu.VMEM((ET, TT), BF16),
        ],
        compiler_params=_cparams(("arbitrary", "arbitrary")),
        name="peer_dense",
    )(x2, *sel, u, vt, g, b, p2, wg, wp)


def _rope_tables():
    inv_freq = ROPE_THETA ** (-jnp.arange(0, ROT_DIM, 2, dtype=F32) / ROT_DIM)
    ang = jnp.arange(SEQ, dtype=F32)[:, None] * inv_freq[None, :]
    cos, sin = jnp.cos(ang), jnp.sin(ang)
    half = ROT_DIM // 2
    pad = HEAD_DIM - ROT_DIM
    c_tab = jnp.concatenate([cos, cos, jnp.ones((SEQ, pad), F32)], axis=1)
    sa_tab = jnp.concatenate([-sin, jnp.zeros((SEQ, half + pad), F32)], axis=1)
    sb_tab = jnp.concatenate([jnp.zeros((SEQ, half), F32), sin, jnp.zeros((SEQ, pad), F32)], axis=1)
    return c_tab, sa_tab, sb_tab


def _trunk(x, p, attn_w_qkv, attn_w_out, conv_w_in, conv_kernel, conv_w_out,
           peer_w_query, peer_sub_keys, peer_u, peer_v,
           ln1_gain, ln1_bias, ln2_gain, ln2_bias, ple_w_proj, ple_w_gate):
    nbatch = x.shape[0]
    t = nbatch * SEQ
    x2 = x.reshape(t, D_MODEL)
    tabs = _rope_tables()
    for i in range(DEPTH):
        j = i // N_MIXERS
        g1 = ln1_gain[i].reshape(1, D_MODEL)
        b1 = ln1_bias[i].reshape(1, D_MODEL)
        if i % N_MIXERS == 0:
            qkv = _qkv_proj(x2, attn_w_qkv[j].astype(BF16))
            o = _attention(qkv.reshape(nbatch, SEQ, N_GROUPS * GROUP_COLS), *tabs)
            x2 = _proj_ln(x2, o.reshape(t, D_MODEL), attn_w_out[j].astype(BF16), g1, b1)
        else:
            x2 = _conv_mixer_ln(x2, conv_w_in[j].astype(BF16), conv_kernel[j],
                                conv_w_out[j].astype(BF16), g1, b1)
        sel = _peer_select(x2, peer_w_query[i].astype(BF16), peer_sub_keys[i].astype(BF16))
        x2 = _peer_dense(
            x2, sel, peer_u[i].astype(BF16), peer_v[i].astype(BF16).T,
            ln2_gain[i].reshape(1, D_MODEL), ln2_bias[i].reshape(1, D_MODEL),
            p[i].reshape(t, PLE_DIM), ple_w_gate[i].astype(BF16), ple_w_proj[i].astype(BF16))
    return x2.reshape(nbatch, SEQ, D_MODEL)


def kernel(x_prompt, x_sample, p_prompt, p_sample, attn_w_qkv, attn_w_out, conv_w_in, conv_kernel, conv_w_out, peer_w_query, peer_sub_keys, peer_u, peer_v, ln1_gain, ln1_bias, ln2_gain, ln2_bias, ple_w_proj, ple_w_gate):
    nb_prompt = x_prompt.shape[0]
    x = jnp.concatenate([x_prompt, x_sample], axis=0)
    p = jnp.concatenate([p_prompt, p_sample], axis=1)
    y = _trunk(x, p, attn_w_qkv, attn_w_out, conv_w_in, conv_kernel, conv_w_out,
               peer_w_query, peer_sub_keys, peer_u, peer_v,
               ln1_gain, ln1_bias, ln2_gain, ln2_bias, ple_w_proj, ple_w_gate)
    return (y[:nb_prompt], y[nb_prompt:])
```

```python
import math

import jax
import jax.numpy as jnp
from jax import lax
from jax.experimental import pallas as pl
from jax.experimental.pallas import tpu as pltpu

F32 = jnp.float32
BF16 = jnp.bfloat16
U32 = jnp.uint32

D_MODEL = 1024
SEQ = 2048
DEPTH = 4
N_MIXERS = 2
HEAD_DIM = 128
N_HEADS = D_MODEL // HEAD_DIM
DILATIONS = (1, 4, 16)
N_GROUPS = 3
GROUP_COLS = 3 * N_HEADS * HEAD_DIM
ROPE_THETA = 500000.0
ROT_DIM = HEAD_DIM // 4
NEG_INF = -1e30
PEER_HEADS = 8
PEER_KEYS = 128
PEER_EXPERTS = PEER_KEYS * PEER_KEYS
PEER_HALF = 128
PEER_TOPK = 16
PLE_DIM = 256
ALPHA = (2.0 * DEPTH) ** 0.25
LN_EPS = 1e-5
INV_SQRT2 = 0.7071067811865476

VMEM_LIMIT_BYTES = 56 * 1024 * 1024
LANES = 128
SUBLANES = 8

TM = 512
QB = 128
HALF_W = 64
TS = 256
TT = 512
DENSE_CHUNKS = 4
ET = PEER_KEYS * SUBLANES
HALO = 8


def _cparams(sem, **flags):
    return pltpu.CompilerParams(dimension_semantics=sem, vmem_limit_bytes=VMEM_LIMIT_BYTES,
                                flags=flags or None)


def _layer_norm(z, g, b):
    mu = jnp.mean(z, axis=-1, keepdims=True)
    zc = z - mu
    var = jnp.mean(zc * zc, axis=-1, keepdims=True)
    return zc * lax.rsqrt(var + LN_EPS) * g + b


def _qkv_kernel(x_ref, w_ref, c_ref, sa_ref, sb_ref, o_ref, xs_scr, *, n_res, rows):
    for r in range(n_res):
        xs_scr[r * rows:(r + 1) * rows, :] = x_ref[:, r * D_MODEL:(r + 1) * D_MODEL].astype(BF16)
    c, sa, sb = c_ref[...], sa_ref[...], sb_ref[...]
    qk_cols = 2 * N_HEADS * HEAD_DIM
    for pair in range(N_HEADS):
        t2 = jnp.dot(xs_scr[...], w_ref[:, pair * 2 * HEAD_DIM:(pair + 1) * 2 * HEAD_DIM],
                     preferred_element_type=F32)
        for hb in range(2):
            t = t2[:, hb * HEAD_DIM:(hb + 1) * HEAD_DIM]
            t = (t * c + pltpu.roll(t, LANES - ROT_DIM // 2, axis=1) * sa
                 + pltpu.roll(t, ROT_DIM // 2, axis=1) * sb)
            cols = slice((2 * pair + hb) * HEAD_DIM, (2 * pair + hb + 1) * HEAD_DIM)
            o_ref[:, cols] = t.astype(o_ref.dtype)
    o_ref[:, qk_cols:] = jnp.dot(
        xs_scr[...], w_ref[:, qk_cols:], preferred_element_type=F32).astype(o_ref.dtype)


def _qkv_proj(x3, w_g, tabs, dil):
    nbatch = x3.shape[0]
    sub_len = SEQ // dil
    rows = min(TM, sub_len)
    n_res = TM // rows
    xv = x3.reshape(nbatch, sub_len, dil * D_MODEL)
    lt_steps = sub_len // rows
    steps = SEQ // TM

    def x_map(b, s):
        return (b, s % lt_steps, s // lt_steps)

    tab = pl.BlockSpec((TM, HEAD_DIM), lambda b, s: (s, 0))
    kern = lambda *refs: _qkv_kernel(*refs, n_res=n_res, rows=rows)
    return pl.pallas_call(
        kern,
        grid=(nbatch, steps),
        in_specs=[
            pl.BlockSpec((None, rows, n_res * D_MODEL), x_map),
            pl.BlockSpec((D_MODEL, GROUP_COLS), lambda b, s: (0, 0)),
            tab, tab, tab,
        ],
        out_specs=pl.BlockSpec((None, TM, GROUP_COLS), lambda b, s: (b, s, 0)),
        out_shape=jax.ShapeDtypeStruct((nbatch, SEQ, GROUP_COLS), BF16),
        scratch_shapes=[pltpu.VMEM((TM, D_MODEL), BF16)],
        compiler_params=_cparams(("arbitrary", "arbitrary")),
        name=f"qkv_proj_d{dil}",
    )(xv, w_g, *tabs)


def _attn_kernel(q0, k0, v0, q1, k1, v1, q2, k2, v2, o_ref, bias_scr, acc_scr, m_scr, l_scr):
    scale = 1.0 / math.sqrt(HEAD_DIM)
    kw_full = QB + 2 * HALF_W
    for case in range(3):
        qi = case * HALF_W + lax.broadcasted_iota(jnp.int32, (QB, kw_full), 0)
        ki = lax.broadcasted_iota(jnp.int32, (QB, kw_full), 1)
        bias_scr[case] = jnp.where(jnp.abs(qi - ki) <= HALF_W, 0.0, NEG_INF)

    nt = (((1,), (1,)), ((), ()))
    for g, ((q_ref, k_ref, v_ref), dil) in enumerate(zip(((q0, k0, v0), (q1, k1, v1), (q2, k2, v2)), DILATIONS)):
        sub_len = SEQ // dil
        nb = sub_len // QB
        kw = min(kw_full, sub_len)
        for bi in range(dil * nb):
            r, mb = divmod(bi, nb)
            m0 = mb * QB
            ws = min(max(m0 - HALF_W, 0), sub_len - kw)
            q = q_ref[bi * QB:(bi + 1) * QB, :]
            k = k_ref[r * sub_len + ws:r * sub_len + ws + kw, :]
            v = v_ref[r * sub_len + ws:r * sub_len + ws + kw, :]
            s = lax.dot_general(q, k, nt, preferred_element_type=F32) * scale
            s = s + bias_scr[(m0 - ws) // HALF_W][:, :kw]
            m = jnp.max(s, axis=-1, keepdims=True)
            p = jnp.exp(s - m)
            den = jnp.sum(p, axis=-1, keepdims=True)
            num = jnp.dot(p.astype(BF16), v, preferred_element_type=F32)
            rows = pl.ds(r + dil * m0, QB, stride=dil) if dil > 1 else pl.ds(m0, QB)
            acc_scr.at[g][rows, :] = num
            m_scr.at[g][rows, :] = jnp.broadcast_to(m, (QB, HEAD_DIM))
            l_scr.at[g][rows, :] = jnp.broadcast_to(den, (QB, HEAD_DIM))

    m_all = jnp.maximum(jnp.maximum(m_scr[0], m_scr[1]), m_scr[2])
    num = jnp.zeros((SEQ, HEAD_DIM), F32)
    den = jnp.zeros((SEQ, HEAD_DIM), F32)
    for g in range(N_GROUPS):
        w = jnp.exp(m_scr[g] - m_all)
        num = num + w * acc_scr[g]
        den = den + w * l_scr[g]
    o_ref[...] = (num / den).astype(o_ref.dtype)


def _attention(qkvs):
    nbatch = qkvs[0].shape[0]

    def col(part):
        return lambda b, h: (b, 0, part * N_HEADS + h)

    in_specs = [pl.BlockSpec((None, SEQ, HEAD_DIM), col(part)) for _ in range(N_GROUPS) for part in range(3)]
    operands = [qkvs[g] for g in range(N_GROUPS) for _ in range(3)]
    return pl.pallas_call(
        _attn_kernel,
        grid=(nbatch, N_HEADS),
        in_specs=in_specs,
        out_specs=pl.BlockSpec((None, SEQ, HEAD_DIM), lambda b, h: (b, 0, h)),
        out_shape=jax.ShapeDtypeStruct((nbatch, SEQ, D_MODEL), BF16),
        scratch_shapes=[
            pltpu.VMEM((3, QB, QB + 2 * HALF_W), F32),
            pltpu.VMEM((N_GROUPS, SEQ, HEAD_DIM), F32),
            pltpu.VMEM((N_GROUPS, SEQ, HEAD_DIM), F32),
            pltpu.VMEM((N_GROUPS, SEQ, HEAD_DIM), F32),
        ],
        compiler_params=_cparams(("arbitrary", "arbitrary")),
        name="dilated_attention",
    )(*operands)


def _proj_ln_kernel(x_ref, o_ref, w_ref, g_ref, b_ref, y_ref):
    h = jnp.dot(o_ref[...], w_ref[...], preferred_element_type=F32)
    y_ref[...] = _layer_norm(ALPHA * x_ref[...] + h, g_ref[...], b_ref[...])


def _proj_ln(x2, o2, w, g, b):
    t = x2.shape[0]
    row = pl.BlockSpec((TM, D_MODEL), lambda i: (i, 0))
    vec = pl.BlockSpec((1, D_MODEL), lambda i: (0, 0))
    return pl.pallas_call(
        _proj_ln_kernel,
        grid=(t // TM,),
        in_specs=[row, row, pl.BlockSpec((D_MODEL, D_MODEL), lambda i: (0, 0)), vec, vec],
        out_specs=row,
        out_shape=jax.ShapeDtypeStruct((t, D_MODEL), F32),
        compiler_params=_cparams(("arbitrary",)),
        name="attn_out_ln",
    )(x2, o2, w, g, b)


def _conv_kernel(xp_ref, x_ref, xn_ref, win_ref, ck_ref, wout_ref, g_ref, b_ref, y_ref, xcat, u_scr):
    i = pl.program_id(0)
    xcat[0:HALO, :] = xp_ref[0].astype(BF16)
    xcat[HALO:HALO + TM, :] = x_ref[...].astype(BF16)
    xcat[HALO + TM:, :] = xn_ref[0].astype(BF16)
    u_scr[...] = jnp.dot(xcat[...], win_ref[...], preferred_element_type=F32)
    hh = u_scr[:, D_MODEL:2 * D_MODEL] * u_scr[:, 2 * D_MODEL:]
    pos = (i * TM + lax.broadcasted_iota(jnp.int32, (TM, 1), 0)) % SEQ
    prev = jnp.where(pos == 0, 0.0, hh[HALO - 1:HALO - 1 + TM, :])
    nxt = jnp.where(pos == SEQ - 1, 0.0, hh[HALO + 1:HALO + 1 + TM, :])
    ck = ck_ref[...]
    conv = prev * ck[0:1, :] + hh[HALO:HALO + TM, :] * ck[1:2, :] + nxt * ck[2:3, :]
    gated = (u_scr[HALO:HALO + TM, 0:D_MODEL] * conv).astype(BF16)
    h = jnp.dot(gated, wout_ref[...], preferred_element_type=F32)
    y_ref[...] = _layer_norm(ALPHA * x_ref[...] + h, g_ref[...], b_ref[...])


def _conv_mixer_ln(x2, w_in, ck, w_out, g, b):
    t = x2.shape[0]
    nblk = t // HALO
    x3 = x2.reshape(nblk, HALO, D_MODEL)
    per = TM // HALO
    row = pl.BlockSpec((TM, D_MODEL), lambda i: (i, 0))
    vec = pl.BlockSpec((1, D_MODEL), lambda i: (0, 0))
    return pl.pallas_call(
        _conv_kernel,
        grid=(t // TM,),
        in_specs=[
            pl.BlockSpec((1, HALO, D_MODEL), lambda i: (jnp.maximum(i * per - 1, 0), 0, 0)),
            row,
            pl.BlockSpec((1, HALO, D_MODEL), lambda i: (jnp.minimum((i + 1) * per, nblk - 1), 0, 0)),
            pl.BlockSpec((D_MODEL, 3 * D_MODEL), lambda i: (0, 0)),
            pl.BlockSpec((3, D_MODEL), lambda i: (0, 0)),
            pl.BlockSpec((D_MODEL, D_MODEL), lambda i: (0, 0)),
            vec, vec,
        ],
        out_specs=row,
        out_shape=jax.ShapeDtypeStruct((t, D_MODEL), F32),
        scratch_shapes=[
            pltpu.VMEM((TM + 2 * HALO, D_MODEL), BF16),
            pltpu.VMEM((TM + 2 * HALO, 3 * D_MODEL), F32),
        ],
        compiler_params=_cparams(("arbitrary",)),
        name="conv_mixer_ln",
    )(x3, x2, x3, w_in, ck, w_out, g, b)


def _batcher_network(n):
    def merge(lo, hi, r):
        step = r * 2
        if step < hi - lo:
            yield from merge(lo, hi, step)
            yield from merge(lo + r, hi, step)
            yield from ((i, i + r) for i in range(lo + r, hi - r, step))
        else:
            yield (lo, lo + r)

    def sort(lo, hi):
        if hi - lo >= 1:
            mid = lo + (hi - lo) // 2
            yield from sort(lo, mid)
            yield from sort(mid + 1, hi)
            yield from merge(lo, hi, 1)

    return tuple(sort(0, n - 1))


_SORT16 = _batcher_network(PEER_TOPK)
_BITONIC16 = tuple((i, i + d) for d in (8, 4, 2, 1) for i in range(PEER_TOPK) if not i & d)
_CAND_ROW_LIMIT = tuple(PEER_TOPK // (a + 1) for a in range(PEER_TOPK))


def _compare_exchange(v, pairs):
    for i, j in pairs:
        v[i], v[j] = jnp.maximum(v[i], v[j]), jnp.minimum(v[i], v[j])


def _top16_sorted(rows):
    v = list(rows)
    _compare_exchange(v, _SORT16)
    for shift in (4, 2, 1):
        other = [pltpu.roll(x, shift, axis=0) for x in v]
        v = [jnp.maximum(v[k], other[PEER_TOPK - 1 - k]) for k in range(PEER_TOPK)]
        _compare_exchange(v, _BITONIC16)
    return v


def _dup16(x):
    hi = pltpu.bitcast(x, U32) & jnp.uint32(0xFFFF0000)
    return hi | (hi >> 16)


def _pack2(lo, hi):
    lo = pltpu.bitcast(lo.astype(BF16).astype(F32), U32)
    hi = pltpu.bitcast(hi.astype(BF16).astype(F32), U32)
    return (hi & jnp.uint32(0xFFFF0000)) | (lo >> 16)


def _select_block(s0, s1):
    nrow = len(s0)
    v0 = _top16_sorted(s0)
    v1 = _top16_sorted(s1)
    lanes = s0[0].shape[1]
    row = lax.broadcasted_iota(jnp.int32, (SUBLANES, lanes), 0)

    def rows_of(vals):
        out = vals[0]
        for r in range(1, SUBLANES):
            out = jnp.where(row == r, vals[r], out)
        return out

    v1_lo, v1_hi, v0_hi = rows_of(v1[:SUBLANES]), rows_of(v1[SUBLANES:]), rows_of(v0[SUBLANES:])
    cands = [v0[0] + v1_lo, v0[0] + v1_hi]
    for a in range(1, SUBLANES):
        cands.append(jnp.where(row < _CAND_ROW_LIMIT[a], v0[a] + v1_lo, -jnp.inf))
    cands.append(v0_hi + v1[0])
    work = list(cands)
    tau = None
    for k in range(PEER_TOPK):
        m = work[0]
        for w in work[1:]:
            m = jnp.maximum(m, w)
        tau = jnp.max(m, axis=0, keepdims=True)
        if k + 1 < PEER_TOPK:
            work = [jnp.where(w == tau, -jnp.inf, w) for w in work]
    top = v0[0] + v1[0]
    sel = [c >= tau for c in cands]
    z = jnp.zeros((1, lanes), F32)
    for c, m in zip(cands, sel):
        z = z + jnp.sum(jnp.where(m, jnp.exp(c - top), 0.0), axis=0, keepdims=True)
    ones = [jnp.where(m, 1.0, 0.0) for m in sel]
    counts = [jnp.sum(ones[0] + ones[1], axis=0, keepdims=True)]
    counts += [jnp.sum(ones[1 + a], axis=0, keepdims=True) for a in range(1, SUBLANES)]
    counts += [ones[-1][r:r + 1, :] for r in range(SUBLANES)]
    cnt = [jnp.zeros((SUBLANES, lanes), F32) for _ in range(nrow)]
    for a in range(PEER_TOPK):
        ca = jnp.broadcast_to(counts[a], (SUBLANES, lanes))
        cnt = [jnp.where(s0[k] == v0[a], ca, cnt[k]) for k in range(nrow)]
    zinv = jnp.broadcast_to(1.0 / z, (SUBLANES, lanes))
    a_fac = [jnp.exp(s0[k] - v0[0]) * zinv for k in range(nrow)]
    b_fac = [jnp.exp(s1[k] - v1[0]) for k in range(nrow)]
    rank1 = [jnp.zeros((SUBLANES, lanes), F32) for _ in range(nrow)]
    for b in range(PEER_TOPK):
        rank1 = [jnp.where(v1[b] > s1[k], float(b + 1), rank1[k]) for k in range(nrow)]
    return cnt, a_fac, rank1, b_fac


def _peer_select_kernel(x_ref, wq_ref, sk_ref, cp_ref, ap_ref, rp_ref, bp_ref):
    xb = x_ref[...].astype(BF16)
    q = jnp.dot(xb, wq_ref[...], preferred_element_type=F32).astype(BF16)
    nt = (((1,), (1,)), ((), ()))
    s0 = lax.dot_general(sk_ref[0], q[:, :PEER_HALF], nt, preferred_element_type=F32)
    s1 = lax.dot_general(sk_ref[1], q[:, PEER_HALF:], nt, preferred_element_type=F32)
    nrow = PEER_KEYS // SUBLANES
    half = nrow // 2
    for tb in range(TS // LANES):
        sl = slice(tb * LANES, (tb + 1) * LANES)
        r0 = [s0[SUBLANES * k:SUBLANES * (k + 1), sl] for k in range(nrow)]
        r1 = [s1[SUBLANES * k:SUBLANES * (k + 1), sl] for k in range(nrow)]
        cnt, a_fac, rank1, b_fac = _select_block(r0, r1)
        for k in range(nrow):
            rows = slice(SUBLANES * k, SUBLANES * (k + 1))
            cp_ref[rows, sl] = _dup16(cnt[k])
            ap_ref[rows, sl] = _dup16(a_fac[k].astype(BF16).astype(F32))
        for k in range(half):
            rows = slice(SUBLANES * k, SUBLANES * (k + 1))
            rp_ref[rows, sl] = _pack2(rank1[k], rank1[k + half])
            bp_ref[rows, sl] = _pack2(b_fac[k], b_fac[k + half])


def _peer_select(x2, wq, sk):
    t = x2.shape[0]
    full = jax.ShapeDtypeStruct((PEER_HEADS, PEER_KEYS, t), U32)
    packed = jax.ShapeDtypeStruct((PEER_HEADS, PEER_KEYS // 2, t), U32)
    fspec = pl.BlockSpec((None, PEER_KEYS, TS), lambda i, h: (h, 0, i))
    pspec = pl.BlockSpec((None, PEER_KEYS // 2, TS), lambda i, h: (h, 0, i))
    return pl.pallas_call(
        _peer_select_kernel,
        grid=(t // TS, PEER_HEADS),
        in_specs=[
            pl.BlockSpec((TS, D_MODEL), lambda i, h: (i, 0)),
            pl.BlockSpec((D_MODEL, 2 * PEER_HALF), lambda i, h: (0, h)),
            pl.BlockSpec((None, 2, PEER_KEYS, PEER_HALF), lambda i, h: (h, 0, 0, 0)),
        ],
        out_specs=[fspec, fspec, pspec, pspec],
        out_shape=[full, full, packed, packed],
        compiler_params=_cparams(("arbitrary", "arbitrary")),
        name="peer_select",
    )(x2, wq, sk)


def _peer_dense_kernel(x_ref, cp_ref, ap_ref, rp_ref, bp_ref, u_ref, vt_ref, vt_prev_ref, g_ref, bb_ref,
                       p_ref, wg_ref, wp_ref, y_ref, *scratch):
    e = pl.program_id(1)
    half = PEER_KEYS // 2
    xt_scr, acc_scr, c_first, c_second = scratch[:4]
    h_scrs = scratch[4:]
    rows_per_chunk = SUBLANES // DENSE_CHUNKS
    ec = rows_per_chunk * PEER_KEYS
    eh = ET // 2

    @pl.when(e == 0)
    def _():
        xt_scr[...] = x_ref[...].T.astype(BF16)
        acc_scr[...] = jnp.zeros(acc_scr.shape, F32)
        c_second[...] = jnp.zeros(c_second.shape, BF16)

    irows = pl.ds(pl.multiple_of(e * SUBLANES, SUBLANES), SUBLANES)

    def gelu(hv):
        return 0.5 * hv * (1.0 + lax.erf(hv * INV_SQRT2))

    def scores(ch):
        h_scrs[ch][...] = jnp.dot(u_ref[ch * ec:(ch + 1) * ec, :], xt_scr[...], preferred_element_type=F32)

    def accumulate(vt_cols, c_scr, part):
        rows = slice(part * (D_MODEL // 2), (part + 1) * (D_MODEL // 2))
        acc_scr[rows, :] += jnp.dot(vt_cols[rows, :], c_scr[...], preferred_element_type=F32)

    def coefficients(ch, c_scr):
        h_scr = h_scrs[ch]
        for ir in range(rows_per_chunk):
            il = ch * rows_per_chunk + ir
            for tb in range(TT // LANES):
                sl = slice(tb * LANES, (tb + 1) * LANES)
                w = jnp.zeros((PEER_KEYS, LANES), BF16)
                for hd in range(PEER_HEADS):
                    cnt_row = jnp.broadcast_to(cp_ref[hd, irows, sl][il:il + 1, :], (half, LANES))
                    a_row = jnp.broadcast_to(ap_ref[hd, irows, sl][il:il + 1, :], (half, LANES))
                    rank1 = pltpu.bitcast(rp_ref[hd, :, sl], BF16)
                    b_fac = pltpu.bitcast(bp_ref[hd, :, sl], BF16)
                    picked = jnp.where(rank1 < pltpu.bitcast(cnt_row, BF16), b_fac, jnp.zeros_like(b_fac))
                    w = w + picked * pltpu.bitcast(a_row, BF16)
                wu = pltpu.bitcast(w, U32)
                w_lo = pltpu.bitcast(wu << 16, F32)
                w_hi = pltpu.bitcast(wu & jnp.uint32(0xFFFF0000), F32)
                r0 = (il * PEER_KEYS) % eh
                h0 = ir * PEER_KEYS
                c_scr[r0:r0 + half, sl] = (w_lo * gelu(h_scr[h0:h0 + half, sl])).astype(BF16)
                c_scr[r0 + half:r0 + PEER_KEYS, sl] = (
                    w_hi * gelu(h_scr[h0 + half:h0 + PEER_KEYS, sl])).astype(BF16)

    assert DENSE_CHUNKS == 4, "the schedule below is written out for four expert chunks"
    scores(0)
    accumulate(vt_prev_ref, c_second, 0)
    coefficients(0, c_first)
    scores(1)
    accumulate(vt_prev_ref, c_second, 1)
    coefficients(1, c_first)
    scores(2)
    accumulate(vt_ref.at[:, :eh], c_first, 0)
    coefficients(2, c_second)
    scores(3)
    accumulate(vt_ref.at[:, :eh], c_first, 1)
    coefficients(3, c_second)

    @pl.when(e == pl.num_programs(1) - 1)
    def _():
        acc_scr[...] += jnp.dot(vt_ref[:, eh:], c_second[...], preferred_element_type=F32)
        f = acc_scr[...].T
        y = _layer_norm(ALPHA * x_ref[...] + f, g_ref[...], bb_ref[...])
        gate = jax.nn.sigmoid(jnp.dot(y.astype(BF16), wg_ref[...], preferred_element_type=F32))
        proj = jnp.dot(p_ref[...].astype(BF16), wp_ref[...], preferred_element_type=F32)
        y_ref[...] = y + gate * proj


def _peer_dense(x2, sel, u, vt, g, b, p2, wg, wp):
    t = x2.shape[0]
    row = pl.BlockSpec((TT, D_MODEL), lambda i, e: (i, 0))
    vec = pl.BlockSpec((1, D_MODEL), lambda i, e: (0, 0))
    full = pl.BlockSpec((PEER_HEADS, PEER_KEYS, TT), lambda i, e: (0, 0, i))
    packed = pl.BlockSpec((PEER_HEADS, PEER_KEYS // 2, TT), lambda i, e: (0, 0, i))
    return pl.pallas_call(
        _peer_dense_kernel,
        grid=(t // TT, PEER_EXPERTS // ET),
        in_specs=[
            row, full, full, packed, packed,
            pl.BlockSpec((ET, D_MODEL), lambda i, e: (e, 0)),
            pl.BlockSpec((D_MODEL, ET), lambda i, e: (0, e)),
            pl.BlockSpec((D_MODEL, ET // 2), lambda i, e: (0, jnp.maximum(2 * e - 1, 0))),
            vec, vec,
            pl.BlockSpec((TT, PLE_DIM), lambda i, e: (i, 0)),
            pl.BlockSpec((D_MODEL, D_MODEL), lambda i, e: (0, 0)),
            pl.BlockSpec((PLE_DIM, D_MODEL), lambda i, e: (0, 0)),
        ],
        out_specs=row,
        out_shape=jax.ShapeDtypeStruct((t, D_MODEL), F32),
        scratch_shapes=(
            [pltpu.VMEM((D_MODEL, TT), BF16),
             pltpu.VMEM((D_MODEL, TT), F32),
             pltpu.VMEM((ET // 2, TT), BF16),
             pltpu.VMEM((ET // 2, TT), BF16)]
            + [pltpu.VMEM((ET // DENSE_CHUNKS, TT), F32)] * DENSE_CHUNKS
        ),
        compiler_params=_cparams(("arbitrary", "arbitrary")),
        name="peer_dense",
    )(x2, *sel, u, vt, vt, g, b, p2, wg, wp)


def _rope_tables(dil):
    inv_freq = ROPE_THETA ** (-jnp.arange(0, ROT_DIM, 2, dtype=F32) / ROT_DIM)
    ang = jnp.arange(SEQ, dtype=F32)[:, None] * inv_freq[None, :]
    cos, sin = jnp.cos(ang), jnp.sin(ang)
    half = ROT_DIM // 2
    pad = HEAD_DIM - ROT_DIM
    c_tab = jnp.concatenate([cos, cos, jnp.ones((SEQ, pad), F32)], axis=1)
    sa_tab = jnp.concatenate([-sin, jnp.zeros((SEQ, half + pad), F32)], axis=1)
    sb_tab = jnp.concatenate([jnp.zeros((SEQ, half), F32), sin, jnp.zeros((SEQ, pad), F32)], axis=1)

    def residue_major(tab):
        return tab.reshape(SEQ // dil, dil, HEAD_DIM).transpose(1, 0, 2).reshape(SEQ, HEAD_DIM)

    return tuple(residue_major(tab) for tab in (c_tab, sa_tab, sb_tab))


def _trunk(x, p, attn_w_qkv, attn_w_out, conv_w_in, conv_kernel, conv_w_out,
           peer_w_query, peer_sub_keys, peer_u, peer_v,
           ln1_gain, ln1_bias, ln2_gain, ln2_bias, ple_w_proj, ple_w_gate):
    nbatch = x.shape[0]
    t = nbatch * SEQ
    x2 = x.reshape(t, D_MODEL)
    tabs = [_rope_tables(dil) for dil in DILATIONS]
    for i in range(DEPTH):
        j = i // N_MIXERS
        g1 = ln1_gain[i].reshape(1, D_MODEL)
        b1 = ln1_bias[i].reshape(1, D_MODEL)
        if i % N_MIXERS == 0:
            w_qkv = attn_w_qkv[j].astype(BF16)
            x3 = x2.reshape(nbatch, SEQ, D_MODEL)
            qkvs = [_qkv_proj(x3, w_qkv[:, g * GROUP_COLS:(g + 1) * GROUP_COLS], tabs[g], dil)
                    for g, dil in enumerate(DILATIONS)]
            o = _attention(qkvs)
            x2 = _proj_ln(x2, o.reshape(t, D_MODEL), attn_w_out[j].astype(BF16), g1, b1)
        else:
            x2 = _conv_mixer_ln(x2, conv_w_in[j].astype(BF16), conv_kernel[j],
                                conv_w_out[j].astype(BF16), g1, b1)
        sel = _peer_select(x2, peer_w_query[i].astype(BF16), peer_sub_keys[i].astype(BF16))
        x2 = _peer_dense(
            x2, sel, peer_u[i].astype(BF16), peer_v[i].astype(BF16).T,
            ln2_gain[i].reshape(1, D_MODEL), ln2_bias[i].reshape(1, D_MODEL),
            p[i].reshape(t, PLE_DIM), ple_w_gate[i].astype(BF16), ple_w_proj[i].astype(BF16))
    return x2.reshape(nbatch, SEQ, D_MODEL)


def kernel(x_prompt, x_sample, p_prompt, p_sample, attn_w_qkv, attn_w_out, conv_w_in, conv_kernel, conv_w_out, peer_w_query, peer_sub_keys, peer_u, peer_v, ln1_gain, ln1_bias, ln2_gain, ln2_bias, ple_w_proj, ple_w_gate):
    nb_prompt = x_prompt.shape[0]
    x = jnp.concatenate([x_prompt, x_sample], axis=0)
    p = jnp.concatenate([p_prompt, p_sample], axis=1)
    y = _trunk(x, p, attn_w_qkv, attn_w_out, conv_w_in, conv_kernel, conv_w_out,
               peer_w_query, peer_sub_keys, peer_u, peer_v,
               ln1_gain, ln1_bias, ln2_gain, ln2_bias, ple_w_proj, ple_w_gate)
    return (y[:nb_prompt], y[nb_prompt:])
```

```python
import functools
import math

import jax
import jax.numpy as jnp
from jax import lax
from jax.experimental import pallas as pl
from jax.experimental.pallas import tpu as pltpu

F32 = jnp.float32
BF16 = jnp.bfloat16
U32 = jnp.uint32

D_MODEL = 1024
SEQ = 2048
DEPTH = 4
N_MIXERS = 2
HEAD_DIM = 128
N_HEADS = D_MODEL // HEAD_DIM
DILATIONS = (1, 4, 16)
N_GROUPS = 3
GROUP_COLS = 3 * N_HEADS * HEAD_DIM
ROPE_THETA = 500000.0
ROT_DIM = HEAD_DIM // 4
NEG_INF = -1e30
PEER_HEADS = 8
PEER_KEYS = 128
PEER_EXPERTS = PEER_KEYS * PEER_KEYS
PEER_HALF = 128
PEER_TOPK = 16
PLE_DIM = 256
ALPHA = (2.0 * DEPTH) ** 0.25
LN_EPS = 1e-5
INV_SQRT2 = 0.7071067811865476

VMEM_LIMIT_BYTES = 56 * 1024 * 1024
LANES = 128
SUBLANES = 8

TM = 512
QB = 128
HALF_W = 64
TS = 512
SELECT_HEADS = 2
TT = 512
DENSE_CHUNKS = 4
ET = PEER_KEYS * SUBLANES
HALO = 8


def _cparams(sem, **flags):
    return pltpu.CompilerParams(dimension_semantics=sem, vmem_limit_bytes=VMEM_LIMIT_BYTES,
                                flags=flags or None)


def _layer_norm(z, g, b):
    mu = jnp.mean(z, axis=-1, keepdims=True)
    zc = z - mu
    var = jnp.mean(zc * zc, axis=-1, keepdims=True)
    return zc * lax.rsqrt(var + LN_EPS) * g + b


def _qkv_kernel(x_ref, w_ref, c_ref, sa_ref, sb_ref, o_ref, xs_scr, *, n_res, rows):
    for r in range(n_res):
        xs_scr[r * rows:(r + 1) * rows, :] = x_ref[:, r * D_MODEL:(r + 1) * D_MODEL].astype(BF16)
    c, sa, sb = c_ref[...], sa_ref[...], sb_ref[...]
    qk_cols = 2 * N_HEADS * HEAD_DIM
    for pair in range(N_HEADS):
        t2 = jnp.dot(xs_scr[...], w_ref[:, pair * 2 * HEAD_DIM:(pair + 1) * 2 * HEAD_DIM],
                     preferred_element_type=F32)
        for hb in range(2):
            t = t2[:, hb * HEAD_DIM:(hb + 1) * HEAD_DIM]
            t = (t * c + pltpu.roll(t, LANES - ROT_DIM // 2, axis=1) * sa
                 + pltpu.roll(t, ROT_DIM // 2, axis=1) * sb)
            cols = slice((2 * pair + hb) * HEAD_DIM, (2 * pair + hb + 1) * HEAD_DIM)
            o_ref[:, cols] = t.astype(o_ref.dtype)
    o_ref[:, qk_cols:] = jnp.dot(
        xs_scr[...], w_ref[:, qk_cols:], preferred_element_type=F32).astype(o_ref.dtype)


def _qkv_proj(x3, w_g, tabs, dil):
    nbatch = x3.shape[0]
    sub_len = SEQ // dil
    rows = min(TM, sub_len)
    n_res = TM // rows
    steps = SEQ // TM
    lt_steps = sub_len // rows
    xv = x3.reshape(nbatch, sub_len, dil * D_MODEL)
    x_spec = pl.BlockSpec((None, rows, n_res * D_MODEL), lambda b, s: (b, s % lt_steps, s // lt_steps))
    tab = pl.BlockSpec((TM, HEAD_DIM), lambda b, s: (s, 0))
    kern = functools.partial(_qkv_kernel, n_res=n_res, rows=rows)
    return pl.pallas_call(
        kern,
        grid=(nbatch, steps),
        in_specs=[
            x_spec,
            pl.BlockSpec((D_MODEL, GROUP_COLS), lambda b, s: (0, 0)),
            tab, tab, tab,
        ],
        out_specs=pl.BlockSpec((None, TM, GROUP_COLS), lambda b, s: (b, s, 0)),
        out_shape=jax.ShapeDtypeStruct((nbatch, SEQ, GROUP_COLS), BF16),
        scratch_shapes=[pltpu.VMEM((TM, D_MODEL), BF16)],
        compiler_params=_cparams(("arbitrary", "arbitrary")),
        name=f"qkv_proj_d{dil}",
    )(xv, w_g, *tabs)


def _attn_kernel(q0, k0, v0, q1, k1, v1, q2, k2, v2, o_ref, bias_scr, acc_scr, m_scr, l_scr):
    scale = 1.0 / math.sqrt(HEAD_DIM)
    kw_full = QB + 2 * HALF_W
    for case in range(3):
        qi = case * HALF_W + lax.broadcasted_iota(jnp.int32, (QB, kw_full), 0)
        ki = lax.broadcasted_iota(jnp.int32, (QB, kw_full), 1)
        bias_scr[case] = jnp.where(jnp.abs(qi - ki) <= HALF_W, 0.0, NEG_INF)

    nt = (((1,), (1,)), ((), ()))
    for g, ((q_ref, k_ref, v_ref), dil) in enumerate(zip(((q0, k0, v0), (q1, k1, v1), (q2, k2, v2)), DILATIONS)):
        sub_len = SEQ // dil
        nb = sub_len // QB
        kw = min(kw_full, sub_len)
        for bi in range(dil * nb):
            r, mb = divmod(bi, nb)
            m0 = mb * QB
            ws = min(max(m0 - HALF_W, 0), sub_len - kw)
            q = q_ref[bi * QB:(bi + 1) * QB, :]
            k = k_ref[r * sub_len + ws:r * sub_len + ws + kw, :]
            v = v_ref[r * sub_len + ws:r * sub_len + ws + kw, :]
            s = lax.dot_general(q, k, nt, preferred_element_type=F32) * scale
            s = s + bias_scr[(m0 - ws) // HALF_W][:, :kw]
            m = jnp.max(s, axis=-1, keepdims=True)
            p = jnp.exp(s - m)
            den = jnp.sum(p, axis=-1, keepdims=True)
            num = jnp.dot(p.astype(BF16), v, preferred_element_type=F32)
            rows = pl.ds(r + dil * m0, QB, stride=dil) if dil > 1 else pl.ds(m0, QB)
            acc_scr.at[g][rows, :] = num
            m_scr.at[g][rows, :] = jnp.broadcast_to(m, (QB, HEAD_DIM))
            l_scr.at[g][rows, :] = jnp.broadcast_to(den, (QB, HEAD_DIM))

    m_all = jnp.maximum(jnp.maximum(m_scr[0], m_scr[1]), m_scr[2])
    num = jnp.zeros((SEQ, HEAD_DIM), F32)
    den = jnp.zeros((SEQ, HEAD_DIM), F32)
    for g in range(N_GROUPS):
        w = jnp.exp(m_scr[g] - m_all)
        num = num + w * acc_scr[g]
        den = den + w * l_scr[g]
    o_ref[...] = (num / den).astype(o_ref.dtype)


def _attention(qkvs):
    nbatch = qkvs[0].shape[0]

    def col(part):
        return lambda b, h: (b, 0, part * N_HEADS + h)

    in_specs = [pl.BlockSpec((None, SEQ, HEAD_DIM), col(part)) for _ in range(N_GROUPS) for part in range(3)]
    operands = [qkvs[g] for g in range(N_GROUPS) for _ in range(3)]
    return pl.pallas_call(
        _attn_kernel,
        grid=(nbatch, N_HEADS),
        in_specs=in_specs,
        out_specs=pl.BlockSpec((None, SEQ, HEAD_DIM), lambda b, h: (b, 0, h)),
        out_shape=jax.ShapeDtypeStruct((nbatch, SEQ, D_MODEL), BF16),
        scratch_shapes=[
            pltpu.VMEM((3, QB, QB + 2 * HALF_W), F32),
            pltpu.VMEM((N_GROUPS, SEQ, HEAD_DIM), F32),
            pltpu.VMEM((N_GROUPS, SEQ, HEAD_DIM), F32),
            pltpu.VMEM((N_GROUPS, SEQ, HEAD_DIM), F32),
        ],
        compiler_params=_cparams(("arbitrary", "arbitrary")),
        name="dilated_attention",
    )(*operands)


def _proj_ln_kernel(x_ref, o_ref, w_ref, g_ref, b_ref, y_ref):
    h = jnp.dot(o_ref[...], w_ref[...], preferred_element_type=F32)
    y_ref[...] = _layer_norm(ALPHA * x_ref[...] + h, g_ref[...], b_ref[...])


def _proj_ln(x2, o2, w, g, b):
    t = x2.shape[0]
    row = pl.BlockSpec((TM, D_MODEL), lambda i: (i, 0))
    vec = pl.BlockSpec((1, D_MODEL), lambda i: (0, 0))
    return pl.pallas_call(
        _proj_ln_kernel,
        grid=(t // TM,),
        in_specs=[row, row, pl.BlockSpec((D_MODEL, D_MODEL), lambda i: (0, 0)), vec, vec],
        out_specs=row,
        out_shape=jax.ShapeDtypeStruct((t, D_MODEL), F32),
        compiler_params=_cparams(("arbitrary",)),
        name="attn_out_ln",
    )(x2, o2, w, g, b)


def _conv_kernel(xp_ref, x_ref, xn_ref, win_ref, ck_ref, wout_ref, g_ref, b_ref, y_ref, xcat, u_scr):
    i = pl.program_id(0)
    xcat[0:HALO, :] = xp_ref[0].astype(BF16)
    xcat[HALO:HALO + TM, :] = x_ref[...].astype(BF16)
    xcat[HALO + TM:, :] = xn_ref[0].astype(BF16)
    u_scr[...] = jnp.dot(xcat[...], win_ref[...], preferred_element_type=F32)
    hh = u_scr[:, D_MODEL:2 * D_MODEL] * u_scr[:, 2 * D_MODEL:]
    pos = (i * TM + lax.broadcasted_iota(jnp.int32, (TM, 1), 0)) % SEQ
    prev = jnp.where(pos == 0, 0.0, hh[HALO - 1:HALO - 1 + TM, :])
    nxt = jnp.where(pos == SEQ - 1, 0.0, hh[HALO + 1:HALO + 1 + TM, :])
    ck = ck_ref[...]
    conv = prev * ck[0:1, :] + hh[HALO:HALO + TM, :] * ck[1:2, :] + nxt * ck[2:3, :]
    gated = (u_scr[HALO:HALO + TM, 0:D_MODEL] * conv).astype(BF16)
    h = jnp.dot(gated, wout_ref[...], preferred_element_type=F32)
    y_ref[...] = _layer_norm(ALPHA * x_ref[...] + h, g_ref[...], b_ref[...])


def _conv_mixer_ln(x2, w_in, ck, w_out, g, b):
    t = x2.shape[0]
    nblk = t // HALO
    x3 = x2.reshape(nblk, HALO, D_MODEL)
    per = TM // HALO
    row = pl.BlockSpec((TM, D_MODEL), lambda i: (i, 0))
    vec = pl.BlockSpec((1, D_MODEL), lambda i: (0, 0))
    return pl.pallas_call(
        _conv_kernel,
        grid=(t // TM,),
        in_specs=[
            pl.BlockSpec((1, HALO, D_MODEL), lambda i: (jnp.maximum(i * per - 1, 0), 0, 0)),
            row,
            pl.BlockSpec((1, HALO, D_MODEL), lambda i: (jnp.minimum((i + 1) * per, nblk - 1), 0, 0)),
            pl.BlockSpec((D_MODEL, 3 * D_MODEL), lambda i: (0, 0)),
            pl.BlockSpec((3, D_MODEL), lambda i: (0, 0)),
            pl.BlockSpec((D_MODEL, D_MODEL), lambda i: (0, 0)),
            vec, vec,
        ],
        out_specs=row,
        out_shape=jax.ShapeDtypeStruct((t, D_MODEL), F32),
        scratch_shapes=[
            pltpu.VMEM((TM + 2 * HALO, D_MODEL), BF16),
            pltpu.VMEM((TM + 2 * HALO, 3 * D_MODEL), F32),
        ],
        compiler_params=_cparams(("arbitrary",)),
        name="conv_mixer_ln",
    )(x3, x2, x3, w_in, ck, w_out, g, b)


def _batcher_network(n):
    def merge(lo, hi, r):
        step = r * 2
        if step < hi - lo:
            yield from merge(lo, hi, step)
            yield from merge(lo + r, hi, step)
            yield from ((i, i + r) for i in range(lo + r, hi - r, step))
        else:
            yield (lo, lo + r)

    def sort(lo, hi):
        if hi - lo >= 1:
            mid = lo + (hi - lo) // 2
            yield from sort(lo, mid)
            yield from sort(mid + 1, hi)
            yield from merge(lo, hi, 1)

    return tuple(sort(0, n - 1))


_SORT16 = _batcher_network(PEER_TOPK)
_BITONIC16 = tuple((i, i + d) for d in (8, 4, 2, 1) for i in range(PEER_TOPK) if not i & d)
_CAND_ROW_LIMIT = tuple(PEER_TOPK // (a + 1) for a in range(PEER_TOPK))


def _compare_exchange(v, pairs):
    for i, j in pairs:
        v[i], v[j] = jnp.maximum(v[i], v[j]), jnp.minimum(v[i], v[j])


def _top16_sorted(rows):
    v = list(rows)
    _compare_exchange(v, _SORT16)
    for shift in (4, 2, 1):
        other = [pltpu.roll(x, shift, axis=0) for x in v]
        v = [jnp.maximum(v[k], other[PEER_TOPK - 1 - k]) for k in range(PEER_TOPK)]
        _compare_exchange(v, _BITONIC16)
    return v


def _dup16(x):
    hi = pltpu.bitcast(x, U32) & jnp.uint32(0xFFFF0000)
    return hi | (hi >> 16)


def _pack2(lo, hi):
    lo = pltpu.bitcast(lo.astype(BF16).astype(F32), U32)
    hi = pltpu.bitcast(hi.astype(BF16).astype(F32), U32)
    return (hi & jnp.uint32(0xFFFF0000)) | (lo >> 16)


def _select_block(s0, s1):
    nrow = len(s0)
    v0 = _top16_sorted(s0)
    v1 = _top16_sorted(s1)
    lanes = s0[0].shape[1]
    row = lax.broadcasted_iota(jnp.int32, (SUBLANES, lanes), 0)

    def rows_of(vals):
        out = vals[0]
        for r in range(1, SUBLANES):
            out = jnp.where(row == r, vals[r], out)
        return out

    v1_lo, v1_hi, v0_hi = rows_of(v1[:SUBLANES]), rows_of(v1[SUBLANES:]), rows_of(v0[SUBLANES:])
    cands = [v0[0] + v1_lo, v0[0] + v1_hi]
    for a in range(1, SUBLANES):
        cands.append(jnp.where(row < _CAND_ROW_LIMIT[a], v0[a] + v1_lo, -jnp.inf))
    cands.append(v0_hi + v1[0])
    work = list(cands)
    tau = None
    for k in range(PEER_TOPK):
        m = work[0]
        for w in work[1:]:
            m = jnp.maximum(m, w)
        tau = jnp.max(m, axis=0, keepdims=True)
        if k + 1 < PEER_TOPK:
            work = [jnp.where(w == tau, -jnp.inf, w) for w in work]
    top = v0[0] + v1[0]
    sel = [c >= tau for c in cands]
    z = jnp.zeros((1, lanes), F32)
    for c, m in zip(cands, sel):
        z = z + jnp.sum(jnp.where(m, jnp.exp(c - top), 0.0), axis=0, keepdims=True)
    ones = [jnp.where(m, 1.0, 0.0) for m in sel]
    counts = [jnp.sum(ones[0] + ones[1], axis=0, keepdims=True)]
    counts += [jnp.sum(ones[1 + a], axis=0, keepdims=True) for a in range(1, SUBLANES)]
    tau_b = jnp.broadcast_to(tau, (SUBLANES, lanes))
    cnt = [jnp.where(s0[k] < v0[SUBLANES - 1],
                     jnp.where(s0[k] >= v0[PEER_TOPK - 1], jnp.where(s0[k] + v1[0] >= tau_b, 1.0, 0.0), 0.0),
                     0.0) for k in range(nrow)]
    for a in range(SUBLANES):
        ca = jnp.broadcast_to(counts[a], (SUBLANES, lanes))
        cnt = [jnp.where(s0[k] == v0[a], ca, cnt[k]) for k in range(nrow)]
    zinv = jnp.broadcast_to(1.0 / z, (SUBLANES, lanes))
    a_fac = [jnp.exp(s0[k] - v0[0]) * zinv for k in range(nrow)]
    b_fac = [jnp.exp(s1[k] - v1[0]) for k in range(nrow)]
    rank1 = [jnp.zeros((SUBLANES, lanes), F32) for _ in range(nrow)]
    for b in range(PEER_TOPK):
        rank1 = [jnp.where(v1[b] > s1[k], float(b + 1), rank1[k]) for k in range(nrow)]
    return cnt, a_fac, rank1, b_fac


def _peer_select_kernel(x_ref, wq_ref, sk_ref, cp_ref, ap_ref, rp_ref, bp_ref):
    xb = x_ref[...].astype(BF16)
    nt = (((1,), (1,)), ((), ()))
    nrow = PEER_KEYS // SUBLANES
    half = nrow // 2
    for hd in range(SELECT_HEADS):
        qcols = slice(hd * 2 * PEER_HALF, (hd + 1) * 2 * PEER_HALF)
        q = jnp.dot(xb, wq_ref[:, qcols], preferred_element_type=F32).astype(BF16)
        s0 = lax.dot_general(sk_ref[hd, 0], q[:, :PEER_HALF], nt, preferred_element_type=F32)
        s1 = lax.dot_general(sk_ref[hd, 1], q[:, PEER_HALF:], nt, preferred_element_type=F32)
        for tb in range(TS // LANES):
            sl = slice(tb * LANES, (tb + 1) * LANES)
            r0 = [s0[SUBLANES * k:SUBLANES * (k + 1), sl] for k in range(nrow)]
            r1 = [s1[SUBLANES * k:SUBLANES * (k + 1), sl] for k in range(nrow)]
            cnt, a_fac, rank1, b_fac = _select_block(r0, r1)
            for k in range(nrow):
                rows = slice(SUBLANES * k, SUBLANES * (k + 1))
                cp_ref[hd, rows, sl] = _dup16(cnt[k])
                ap_ref[hd, rows, sl] = _dup16(a_fac[k].astype(BF16).astype(F32))
            for k in range(half):
                rows = slice(SUBLANES * k, SUBLANES * (k + 1))
                rp_ref[hd, rows, sl] = _pack2(rank1[k], rank1[k + half])
                bp_ref[hd, rows, sl] = _pack2(b_fac[k], b_fac[k + half])


def _peer_select(x2, wq, sk):
    t = x2.shape[0]
    full = jax.ShapeDtypeStruct((PEER_HEADS, PEER_KEYS, t), U32)
    packed = jax.ShapeDtypeStruct((PEER_HEADS, PEER_KEYS // 2, t), U32)
    fspec = pl.BlockSpec((SELECT_HEADS, PEER_KEYS, TS), lambda i, h: (h, 0, i))
    pspec = pl.BlockSpec((SELECT_HEADS, PEER_KEYS // 2, TS), lambda i, h: (h, 0, i))
    return pl.pallas_call(
        _peer_select_kernel,
        grid=(t // TS, PEER_HEADS // SELECT_HEADS),
        in_specs=[
            pl.BlockSpec((TS, D_MODEL), lambda i, h: (i, 0)),
            pl.BlockSpec((D_MODEL, SELECT_HEADS * 2 * PEER_HALF), lambda i, h: (0, h)),
            pl.BlockSpec((SELECT_HEADS, 2, PEER_KEYS, PEER_HALF), lambda i, h: (h, 0, 0, 0)),
        ],
        out_specs=[fspec, fspec, pspec, pspec],
        out_shape=[full, full, packed, packed],
        compiler_params=_cparams(("arbitrary", "arbitrary")),
        name="peer_select",
    )(x2, wq, sk)


def _peer_dense_kernel(x_ref, cp_ref, ap_ref, rp_ref, bp_ref, u_ref, vt_ref, vt_prev_ref, g_ref, bb_ref,
                       p_ref, wg_ref, wp_ref, y_ref, *scratch):
    e = pl.program_id(1)
    half = PEER_KEYS // 2
    xt_scr, acc_scr, c_first, c_second = scratch[:4]
    h_scrs = scratch[4:]
    rows_per_chunk = SUBLANES // DENSE_CHUNKS
    ec = rows_per_chunk * PEER_KEYS
    eh = ET // 2

    @pl.when(e == 0)
    def _():
        xt_scr[...] = x_ref[...].T.astype(BF16)
        acc_scr[...] = jnp.zeros(acc_scr.shape, F32)
        c_second[...] = jnp.zeros(c_second.shape, BF16)

    irows = pl.ds(pl.multiple_of(e * SUBLANES, SUBLANES), SUBLANES)

    def gelu(hv):
        return 0.5 * hv * (1.0 + lax.erf(hv * INV_SQRT2))

    def scores(ch):
        h_scrs[ch][...] = jnp.dot(u_ref[ch * ec:(ch + 1) * ec, :], xt_scr[...], preferred_element_type=F32)

    pieces = DENSE_CHUNKS // 2

    def accumulate(vt_cols, c_scr, part):
        rows = slice(part * (D_MODEL // pieces), (part + 1) * (D_MODEL // pieces))
        acc_scr[rows, :] += jnp.dot(vt_cols[rows, :], c_scr[...], preferred_element_type=F32)

    def coefficients(ch, c_scr):
        h_scr = h_scrs[ch]
        for tb in range(TT // LANES):
            sl = slice(tb * LANES, (tb + 1) * LANES)
            ws = [jnp.zeros((PEER_KEYS, LANES), BF16) for _ in range(rows_per_chunk)]
            for hd in range(PEER_HEADS):
                rank1 = pltpu.bitcast(rp_ref[hd, :, sl], BF16)
                b_fac = pltpu.bitcast(bp_ref[hd, :, sl], BF16)
                cnt8 = cp_ref[hd, irows, sl]
                a8 = ap_ref[hd, irows, sl]
                for ir in range(rows_per_chunk):
                    il = ch * rows_per_chunk + ir
                    cnt_row = jnp.broadcast_to(cnt8[il:il + 1, :], (half, LANES))
                    a_row = jnp.broadcast_to(a8[il:il + 1, :], (half, LANES))
                    picked = jnp.where(rank1 < pltpu.bitcast(cnt_row, BF16), b_fac, jnp.zeros_like(b_fac))
                    ws[ir] = ws[ir] + picked * pltpu.bitcast(a_row, BF16)
            for ir in range(rows_per_chunk):
                il = ch * rows_per_chunk + ir
                wu = pltpu.bitcast(ws[ir], U32)
                w_lo = pltpu.bitcast(wu << 16, F32)
                w_hi = pltpu.bitcast(wu & jnp.uint32(0xFFFF0000), F32)
                r0 = (il * PEER_KEYS) % eh
                h0 = ir * PEER_KEYS
                c_scr[r0:r0 + half, sl] = (w_lo * gelu(h_scr[h0:h0 + half, sl])).astype(BF16)
                c_scr[r0 + half:r0 + PEER_KEYS, sl] = (
                    w_hi * gelu(h_scr[h0 + half:h0 + PEER_KEYS, sl])).astype(BF16)

    scores(0)
    for ch in range(DENSE_CHUNKS):
        if ch + 1 < DENSE_CHUNKS:
            scores(ch + 1)
        if ch < pieces:
            accumulate(vt_prev_ref, c_second, ch)
            coefficients(ch, c_first)
        else:
            accumulate(vt_ref.at[:, :eh], c_first, ch - pieces)
            coefficients(ch, c_second)

    @pl.when(e == pl.num_programs(1) - 1)
    def _():
        acc_scr[...] += jnp.dot(vt_ref[:, eh:], c_second[...], preferred_element_type=F32)
        f = acc_scr[...].T
        y = _layer_norm(ALPHA * x_ref[...] + f, g_ref[...], bb_ref[...])
        gate = jax.nn.sigmoid(jnp.dot(y.astype(BF16), wg_ref[...], preferred_element_type=F32))
        proj = jnp.dot(p_ref[...].astype(BF16), wp_ref[...], preferred_element_type=F32)
        y_ref[...] = y + gate * proj


def _peer_dense(x2, sel, u, vt, g, b, p2, wg, wp):
    t = x2.shape[0]
    row = pl.BlockSpec((TT, D_MODEL), lambda i, e: (i, 0))
    vec = pl.BlockSpec((1, D_MODEL), lambda i, e: (0, 0))
    full = pl.BlockSpec((PEER_HEADS, PEER_KEYS, TT), lambda i, e: (0, 0, i))
    packed = pl.BlockSpec((PEER_HEADS, PEER_KEYS // 2, TT), lambda i, e: (0, 0, i))
    return pl.pallas_call(
        _peer_dense_kernel,
        grid=(t // TT, PEER_EXPERTS // ET),
        in_specs=[
            row, full, full, packed, packed,
            pl.BlockSpec((ET, D_MODEL), lambda i, e: (e, 0)),
            pl.BlockSpec((D_MODEL, ET), lambda i, e: (0, e)),
            pl.BlockSpec((D_MODEL, ET // 2), lambda i, e: (0, jnp.maximum(2 * e - 1, 0))),
            vec, vec,
            pl.BlockSpec((TT, PLE_DIM), lambda i, e: (i, 0)),
            pl.BlockSpec((D_MODEL, D_MODEL), lambda i, e: (0, 0)),
            pl.BlockSpec((PLE_DIM, D_MODEL), lambda i, e: (0, 0)),
        ],
        out_specs=row,
        out_shape=jax.ShapeDtypeStruct((t, D_MODEL), F32),
        scratch_shapes=(
            [pltpu.VMEM((D_MODEL, TT), BF16),
             pltpu.VMEM((D_MODEL, TT), F32),
             pltpu.VMEM((ET // 2, TT), BF16),
             pltpu.VMEM((ET // 2, TT), BF16)]
            + [pltpu.VMEM((ET // DENSE_CHUNKS, TT), F32)] * DENSE_CHUNKS
        ),
        compiler_params=_cparams(("arbitrary", "arbitrary")),
        name="peer_dense",
    )(x2, *sel, u, vt, vt, g, b, p2, wg, wp)


def _rope_tables(dil):
    inv_freq = ROPE_THETA ** (-jnp.arange(0, ROT_DIM, 2, dtype=F32) / ROT_DIM)
    ang = jnp.arange(SEQ, dtype=F32)[:, None] * inv_freq[None, :]
    cos, sin = jnp.cos(ang), jnp.sin(ang)
    half = ROT_DIM // 2
    pad = HEAD_DIM - ROT_DIM
    c_tab = jnp.concatenate([cos, cos, jnp.ones((SEQ, pad), F32)], axis=1)
    sa_tab = jnp.concatenate([-sin, jnp.zeros((SEQ, half + pad), F32)], axis=1)
    sb_tab = jnp.concatenate([jnp.zeros((SEQ, half), F32), sin, jnp.zeros((SEQ, pad), F32)], axis=1)

    def residue_major(tab):
        return tab.reshape(SEQ // dil, dil, HEAD_DIM).transpose(1, 0, 2).reshape(SEQ, HEAD_DIM)

    return tuple(residue_major(tab) for tab in (c_tab, sa_tab, sb_tab))


def _trunk(x, p, attn_w_qkv, attn_w_out, conv_w_in, conv_kernel, conv_w_out,
           peer_w_query, peer_sub_keys, peer_u, peer_vt,
           ln1_gain, ln1_bias, ln2_gain, ln2_bias, ple_w_proj, ple_w_gate):
    nbatch = x.shape[0]
    t = nbatch * SEQ
    x2 = x.reshape(t, D_MODEL)
    tabs = [_rope_tables(dil) for dil in DILATIONS]
    for i in range(DEPTH):
        j = i // N_MIXERS
        g1 = ln1_gain[i].reshape(1, D_MODEL)
        b1 = ln1_bias[i].reshape(1, D_MODEL)
        if i % N_MIXERS == 0:
            w_qkv = attn_w_qkv[j]
            x3 = x2.reshape(nbatch, SEQ, D_MODEL)
            qkvs = [_qkv_proj(x3, w_qkv[:, g * GROUP_COLS:(g + 1) * GROUP_COLS], tabs[g], dil)
                    for g, dil in enumerate(DILATIONS)]
            o = _attention(qkvs)
            x2 = _proj_ln(x2, o.reshape(t, D_MODEL), attn_w_out[j], g1, b1)
        else:
            x2 = _conv_mixer_ln(x2, conv_w_in[j], conv_kernel[j], conv_w_out[j], g1, b1)
        sel = _peer_select(x2, peer_w_query[i], peer_sub_keys[i])
        x2 = _peer_dense(
            x2, sel, peer_u[i], peer_vt[i],
            ln2_gain[i].reshape(1, D_MODEL), ln2_bias[i].reshape(1, D_MODEL),
            p[i].reshape(t, PLE_DIM), ple_w_gate[i], ple_w_proj[i])
    return x2.reshape(nbatch, SEQ, D_MODEL)


def kernel(x_prompt, x_sample, p_prompt, p_sample, attn_w_qkv, attn_w_out, conv_w_in, conv_kernel, conv_w_out, peer_w_query, peer_sub_keys, peer_u, peer_v, ln1_gain, ln1_bias, ln2_gain, ln2_bias, ple_w_proj, ple_w_gate):
    weights = (attn_w_qkv.astype(BF16), attn_w_out.astype(BF16), conv_w_in.astype(BF16), conv_kernel,
               conv_w_out.astype(BF16), peer_w_query.astype(BF16), peer_sub_keys.astype(BF16),
               peer_u.astype(BF16), peer_v.astype(BF16).transpose(0, 2, 1),
               ln1_gain, ln1_bias, ln2_gain, ln2_bias, ple_w_proj.astype(BF16), ple_w_gate.astype(BF16))
    return (_trunk(x_prompt, p_prompt, *weights), _trunk(x_sample, p_sample, *weights))
```

```python
import functools
import math

import jax
import jax.numpy as jnp
from jax import lax
from jax.experimental import pallas as pl
from jax.experimental.pallas import tpu as pltpu

F32 = jnp.float32
BF16 = jnp.bfloat16
U32 = jnp.uint32

D_MODEL = 1024
SEQ = 2048
DEPTH = 4
N_MIXERS = 2
HEAD_DIM = 128
N_HEADS = D_MODEL // HEAD_DIM
DILATIONS = (1, 4, 16)
N_GROUPS = 3
GROUP_COLS = 3 * N_HEADS * HEAD_DIM
ROPE_THETA = 500000.0
ROT_DIM = HEAD_DIM // 4
NEG_INF = -1e30
PEER_HEADS = 8
PEER_KEYS = 128
PEER_EXPERTS = PEER_KEYS * PEER_KEYS
PEER_HALF = 128
PEER_TOPK = 16
PLE_DIM = 256
ALPHA = (2.0 * DEPTH) ** 0.25
LN_EPS = 1e-5
INV_SQRT2 = 0.7071067811865476

VMEM_LIMIT_BYTES = 56 * 1024 * 1024
LANES = 128
SUBLANES = 8

TM = 512
QB = 128
HALF_W = 64
TS = 512
SELECT_HEADS = 2
TT = 512
DENSE_CHUNKS = 4
ET = PEER_KEYS * SUBLANES
HALO = 8


def _cparams(sem, **flags):
    return pltpu.CompilerParams(dimension_semantics=sem, vmem_limit_bytes=VMEM_LIMIT_BYTES,
                                flags=flags or None)


def _layer_norm(z, g, b):
    mu = jnp.mean(z, axis=-1, keepdims=True)
    zc = z - mu
    var = jnp.mean(zc * zc, axis=-1, keepdims=True)
    return zc * lax.rsqrt(var + LN_EPS) * g + b


def _qkv_kernel(x_ref, w_ref, c_ref, sa_ref, sb_ref, o_ref, xs_scr, *, n_res, rows):
    for r in range(n_res):
        xs_scr[r * rows:(r + 1) * rows, :] = x_ref[:, r * D_MODEL:(r + 1) * D_MODEL].astype(BF16)
    c, sa, sb = c_ref[...], sa_ref[...], sb_ref[...]
    qk_cols = 2 * N_HEADS * HEAD_DIM
    for pair in range(N_HEADS):
        t2 = jnp.dot(xs_scr[...], w_ref[:, pair * 2 * HEAD_DIM:(pair + 1) * 2 * HEAD_DIM],
                     preferred_element_type=F32)
        for hb in range(2):
            t = t2[:, hb * HEAD_DIM:(hb + 1) * HEAD_DIM]
            t = (t * c + pltpu.roll(t, LANES - ROT_DIM // 2, axis=1) * sa
                 + pltpu.roll(t, ROT_DIM // 2, axis=1) * sb)
            cols = slice((2 * pair + hb) * HEAD_DIM, (2 * pair + hb + 1) * HEAD_DIM)
            o_ref[:, cols] = t.astype(o_ref.dtype)
    o_ref[:, qk_cols:] = jnp.dot(
        xs_scr[...], w_ref[:, qk_cols:], preferred_element_type=F32).astype(o_ref.dtype)


def _qkv_proj(x3, w_all, layer, group, tabs, dil):
    nbatch = x3.shape[0]
    sub_len = SEQ // dil
    rows = min(TM, sub_len)
    n_res = TM // rows
    steps = SEQ // TM
    lt_steps = sub_len // rows
    xv = x3.reshape(nbatch, sub_len, dil * D_MODEL)
    x_spec = pl.BlockSpec((None, rows, n_res * D_MODEL), lambda b, s: (b, s % lt_steps, s // lt_steps))
    tab = pl.BlockSpec((TM, HEAD_DIM), lambda b, s: (s, 0))
    kern = functools.partial(_qkv_kernel, n_res=n_res, rows=rows)
    return pl.pallas_call(
        kern,
        grid=(nbatch, steps),
        in_specs=[
            x_spec,
            pl.BlockSpec((None, D_MODEL, GROUP_COLS), lambda b, s: (layer, 0, group)),
            tab, tab, tab,
        ],
        out_specs=pl.BlockSpec((None, TM, GROUP_COLS), lambda b, s: (b, s, 0)),
        out_shape=jax.ShapeDtypeStruct((nbatch, SEQ, GROUP_COLS), BF16),
        scratch_shapes=[pltpu.VMEM((TM, D_MODEL), BF16)],
        compiler_params=_cparams(("arbitrary", "arbitrary")),
        name=f"qkv_proj_d{dil}",
    )(xv, w_all, *tabs)


def _attn_kernel(q0, k0, v0, q1, k1, v1, q2, k2, v2, o_ref, bias_scr, acc_scr, m_scr, l_scr):
    scale = 1.0 / math.sqrt(HEAD_DIM)
    kw_full = QB + 2 * HALF_W
    for case in range(3):
        qi = case * HALF_W + lax.broadcasted_iota(jnp.int32, (QB, kw_full), 0)
        ki = lax.broadcasted_iota(jnp.int32, (QB, kw_full), 1)
        bias_scr[case] = jnp.where(jnp.abs(qi - ki) <= HALF_W, 0.0, NEG_INF)

    nt = (((1,), (1,)), ((), ()))
    for g, ((q_ref, k_ref, v_ref), dil) in enumerate(zip(((q0, k0, v0), (q1, k1, v1), (q2, k2, v2)), DILATIONS)):
        sub_len = SEQ // dil
        nb = sub_len // QB
        kw = min(kw_full, sub_len)
        for bi in range(dil * nb):
            r, mb = divmod(bi, nb)
            m0 = mb * QB
            ws = min(max(m0 - HALF_W, 0), sub_len - kw)
            q = q_ref[bi * QB:(bi + 1) * QB, :]
            k = k_ref[r * sub_len + ws:r * sub_len + ws + kw, :]
            v = v_ref[r * sub_len + ws:r * sub_len + ws + kw, :]
            s = lax.dot_general(q, k, nt, preferred_element_type=F32) * scale
            s = s + bias_scr[(m0 - ws) // HALF_W][:, :kw]
            m = jnp.max(s, axis=-1, keepdims=True)
            p = jnp.exp(s - m)
            den = jnp.sum(p, axis=-1, keepdims=True)
            num = jnp.dot(p.astype(BF16), v, preferred_element_type=F32)
            rows = pl.ds(r + dil * m0, QB, stride=dil) if dil > 1 else pl.ds(m0, QB)
            acc_scr.at[g][rows, :] = num
            m_scr.at[g][rows, :] = jnp.broadcast_to(m, (QB, HEAD_DIM))
            l_scr.at[g][rows, :] = jnp.broadcast_to(den, (QB, HEAD_DIM))

    m_all = jnp.maximum(jnp.maximum(m_scr[0], m_scr[1]), m_scr[2])
    num = jnp.zeros((SEQ, HEAD_DIM), F32)
    den = jnp.zeros((SEQ, HEAD_DIM), F32)
    for g in range(N_GROUPS):
        w = jnp.exp(m_scr[g] - m_all)
        num = num + w * acc_scr[g]
        den = den + w * l_scr[g]
    o_ref[...] = (num / den).astype(o_ref.dtype)


def _attention(qkvs):
    nbatch = qkvs[0].shape[0]

    def col(part):
        return lambda b, h: (b, 0, part * N_HEADS + h)

    in_specs = [pl.BlockSpec((None, SEQ, HEAD_DIM), col(part)) for _ in range(N_GROUPS) for part in range(3)]
    operands = [qkvs[g] for g in range(N_GROUPS) for _ in range(3)]
    return pl.pallas_call(
        _attn_kernel,
        grid=(nbatch, N_HEADS),
        in_specs=in_specs,
        out_specs=pl.BlockSpec((None, SEQ, HEAD_DIM), lambda b, h: (b, 0, h)),
        out_shape=jax.ShapeDtypeStruct((nbatch, SEQ, D_MODEL), BF16),
        scratch_shapes=[
            pltpu.VMEM((3, QB, QB + 2 * HALF_W), F32),
            pltpu.VMEM((N_GROUPS, SEQ, HEAD_DIM), F32),
            pltpu.VMEM((N_GROUPS, SEQ, HEAD_DIM), F32),
            pltpu.VMEM((N_GROUPS, SEQ, HEAD_DIM), F32),
        ],
        compiler_params=_cparams(("arbitrary", "arbitrary")),
        name="dilated_attention",
    )(*operands)


def _proj_ln_kernel(x_ref, o_ref, w_ref, g_ref, b_ref, y_ref):
    h = jnp.dot(o_ref[...], w_ref[...], preferred_element_type=F32)
    y_ref[...] = _layer_norm(ALPHA * x_ref[...] + h, g_ref[...], b_ref[...])


def _proj_ln(x2, o2, w, g, b):
    t = x2.shape[0]
    row = pl.BlockSpec((TM, D_MODEL), lambda i: (i, 0))
    vec = pl.BlockSpec((1, D_MODEL), lambda i: (0, 0))
    return pl.pallas_call(
        _proj_ln_kernel,
        grid=(t // TM,),
        in_specs=[row, row, pl.BlockSpec((D_MODEL, D_MODEL), lambda i: (0, 0)), vec, vec],
        out_specs=row,
        out_shape=jax.ShapeDtypeStruct((t, D_MODEL), F32),
        compiler_params=_cparams(("arbitrary",)),
        name="attn_out_ln",
    )(x2, o2, w, g, b)


def _conv_kernel(xp_ref, x_ref, xn_ref, win_ref, ck_ref, wout_ref, g_ref, b_ref, y_ref, xcat, u_scr):
    i = pl.program_id(0)
    xcat[0:HALO, :] = xp_ref[0].astype(BF16)
    xcat[HALO:HALO + TM, :] = x_ref[...].astype(BF16)
    xcat[HALO + TM:, :] = xn_ref[0].astype(BF16)
    u_scr[...] = jnp.dot(xcat[...], win_ref[...], preferred_element_type=F32)
    hh = u_scr[:, D_MODEL:2 * D_MODEL] * u_scr[:, 2 * D_MODEL:]
    pos = (i * TM + lax.broadcasted_iota(jnp.int32, (TM, 1), 0)) % SEQ
    prev = jnp.where(pos == 0, 0.0, hh[HALO - 1:HALO - 1 + TM, :])
    nxt = jnp.where(pos == SEQ - 1, 0.0, hh[HALO + 1:HALO + 1 + TM, :])
    ck = ck_ref[...]
    conv = prev * ck[0:1, :] + hh[HALO:HALO + TM, :] * ck[1:2, :] + nxt * ck[2:3, :]
    gated = (u_scr[HALO:HALO + TM, 0:D_MODEL] * conv).astype(BF16)
    h = jnp.dot(gated, wout_ref[...], preferred_element_type=F32)
    y_ref[...] = _layer_norm(ALPHA * x_ref[...] + h, g_ref[...], b_ref[...])


def _conv_mixer_ln(x2, w_in, ck, w_out, g, b):
    t = x2.shape[0]
    nblk = t // HALO
    x3 = x2.reshape(nblk, HALO, D_MODEL)
    per = TM // HALO
    row = pl.BlockSpec((TM, D_MODEL), lambda i: (i, 0))
    vec = pl.BlockSpec((1, D_MODEL), lambda i: (0, 0))
    return pl.pallas_call(
        _conv_kernel,
        grid=(t // TM,),
        in_specs=[
            pl.BlockSpec((1, HALO, D_MODEL), lambda i: (jnp.maximum(i * per - 1, 0), 0, 0)),
            row,
            pl.BlockSpec((1, HALO, D_MODEL), lambda i: (jnp.minimum((i + 1) * per, nblk - 1), 0, 0)),
            pl.BlockSpec((D_MODEL, 3 * D_MODEL), lambda i: (0, 0)),
            pl.BlockSpec((3, D_MODEL), lambda i: (0, 0)),
            pl.BlockSpec((D_MODEL, D_MODEL), lambda i: (0, 0)),
            vec, vec,
        ],
        out_specs=row,
        out_shape=jax.ShapeDtypeStruct((t, D_MODEL), F32),
        scratch_shapes=[
            pltpu.VMEM((TM + 2 * HALO, D_MODEL), BF16),
            pltpu.VMEM((TM + 2 * HALO, 3 * D_MODEL), F32),
        ],
        compiler_params=_cparams(("arbitrary",)),
        name="conv_mixer_ln",
    )(x3, x2, x3, w_in, ck, w_out, g, b)


def _batcher_network(n):
    def merge(lo, hi, r):
        step = r * 2
        if step < hi - lo:
            yield from merge(lo, hi, step)
            yield from merge(lo + r, hi, step)
            yield from ((i, i + r) for i in range(lo + r, hi - r, step))
        else:
            yield (lo, lo + r)

    def sort(lo, hi):
        if hi - lo >= 1:
            mid = lo + (hi - lo) // 2
            yield from sort(lo, mid)
            yield from sort(mid + 1, hi)
            yield from merge(lo, hi, 1)

    return tuple(sort(0, n - 1))


_SORT16 = _batcher_network(PEER_TOPK)
_BITONIC16 = tuple((i, i + d) for d in (8, 4, 2, 1) for i in range(PEER_TOPK) if not i & d)
_CAND_ROW_LIMIT = tuple(PEER_TOPK // (a + 1) for a in range(PEER_TOPK))


def _compare_exchange(v, pairs):
    for i, j in pairs:
        v[i], v[j] = jnp.maximum(v[i], v[j]), jnp.minimum(v[i], v[j])


def _top16_sorted(rows):
    v = list(rows)
    _compare_exchange(v, _SORT16)
    for shift in (4, 2, 1):
        other = [pltpu.roll(x, shift, axis=0) for x in v]
        v = [jnp.maximum(v[k], other[PEER_TOPK - 1 - k]) for k in range(PEER_TOPK)]
        _compare_exchange(v, _BITONIC16)
    return v


def _dup16(x):
    hi = pltpu.bitcast(x, U32) & jnp.uint32(0xFFFF0000)
    return hi | (hi >> 16)


def _pack2(lo, hi):
    lo = pltpu.bitcast(lo.astype(BF16).astype(F32), U32)
    hi = pltpu.bitcast(hi.astype(BF16).astype(F32), U32)
    return (hi & jnp.uint32(0xFFFF0000)) | (lo >> 16)


def _select_block(s0, s1):
    nrow = len(s0)
    v0 = _top16_sorted(s0)
    v1 = _top16_sorted(s1)
    lanes = s0[0].shape[1]
    row = lax.broadcasted_iota(jnp.int32, (SUBLANES, lanes), 0)

    def rows_of(vals):
        out = vals[0]
        for r in range(1, SUBLANES):
            out = jnp.where(row == r, vals[r], out)
        return out

    v1_lo, v1_hi, v0_hi = rows_of(v1[:SUBLANES]), rows_of(v1[SUBLANES:]), rows_of(v0[SUBLANES:])
    cands = [v0[0] + v1_lo, v0[0] + v1_hi]
    for a in range(1, SUBLANES):
        cands.append(jnp.where(row < _CAND_ROW_LIMIT[a], v0[a] + v1_lo, -jnp.inf))
    cands.append(v0_hi + v1[0])
    work = list(cands)
    tau = None
    for k in range(PEER_TOPK):
        m = work[0]
        for w in work[1:]:
            m = jnp.maximum(m, w)
        tau = jnp.max(m, axis=0, keepdims=True)
        if k + 1 < PEER_TOPK:
            work = [jnp.where(w == tau, -jnp.inf, w) for w in work]
    top = v0[0] + v1[0]
    sel = [c >= tau for c in cands]
    z = jnp.zeros((1, lanes), F32)
    for c, m in zip(cands, sel):
        z = z + jnp.sum(jnp.where(m, jnp.exp(c - top), 0.0), axis=0, keepdims=True)
    ones = [jnp.where(m, 1.0, 0.0) for m in sel]
    counts = [jnp.sum(ones[0] + ones[1], axis=0, keepdims=True)]
    counts += [jnp.sum(ones[1 + a], axis=0, keepdims=True) for a in range(1, SUBLANES)]
    tau_b = jnp.broadcast_to(tau, (SUBLANES, lanes))
    cnt = [jnp.where(s0[k] < v0[SUBLANES - 1],
                     jnp.where(s0[k] >= v0[PEER_TOPK - 1], jnp.where(s0[k] + v1[0] >= tau_b, 1.0, 0.0), 0.0),
                     0.0) for k in range(nrow)]
    for a in range(SUBLANES):
        ca = jnp.broadcast_to(counts[a], (SUBLANES, lanes))
        cnt = [jnp.where(s0[k] == v0[a], ca, cnt[k]) for k in range(nrow)]
    zinv = jnp.broadcast_to(1.0 / z, (SUBLANES, lanes))
    a_fac = [jnp.exp(s0[k] - v0[0]) * zinv for k in range(nrow)]
    b_fac = [jnp.exp(s1[k] - v1[0]) for k in range(nrow)]
    rank1 = [jnp.zeros((SUBLANES, lanes), F32) for _ in range(nrow)]
    for b in range(PEER_TOPK):
        rank1 = [jnp.where(v1[b] > s1[k], float(b + 1), rank1[k]) for k in range(nrow)]
    return cnt, a_fac, rank1, b_fac


def _peer_select_kernel(x_ref, wq_ref, sk_ref, cp_ref, ap_ref, rp_ref, bp_ref):
    xb = x_ref[...].astype(BF16)
    nt = (((1,), (1,)), ((), ()))
    nrow = PEER_KEYS // SUBLANES
    half = nrow // 2
    for hd in range(SELECT_HEADS):
        qcols = slice(hd * 2 * PEER_HALF, (hd + 1) * 2 * PEER_HALF)
        q = jnp.dot(xb, wq_ref[:, qcols], preferred_element_type=F32).astype(BF16)
        s0 = lax.dot_general(sk_ref[hd, 0], q[:, :PEER_HALF], nt, preferred_element_type=F32)
        s1 = lax.dot_general(sk_ref[hd, 1], q[:, PEER_HALF:], nt, preferred_element_type=F32)
        for tb in range(TS // LANES):
            sl = slice(tb * LANES, (tb + 1) * LANES)
            r0 = [s0[SUBLANES * k:SUBLANES * (k + 1), sl] for k in range(nrow)]
            r1 = [s1[SUBLANES * k:SUBLANES * (k + 1), sl] for k in range(nrow)]
            cnt, a_fac, rank1, b_fac = _select_block(r0, r1)
            for k in range(nrow):
                rows = slice(SUBLANES * k, SUBLANES * (k + 1))
                cp_ref[hd, rows, sl] = _dup16(cnt[k])
                ap_ref[hd, rows, sl] = _dup16(a_fac[k].astype(BF16).astype(F32))
            for k in range(half):
                rows = slice(SUBLANES * k, SUBLANES * (k + 1))
                rp_ref[hd, rows, sl] = _pack2(rank1[k], rank1[k + half])
                bp_ref[hd, rows, sl] = _pack2(b_fac[k], b_fac[k + half])


def _peer_select(x2, wq, sk):
    t = x2.shape[0]
    full = jax.ShapeDtypeStruct((PEER_HEADS, PEER_KEYS, t), U32)
    packed = jax.ShapeDtypeStruct((PEER_HEADS, PEER_KEYS // 2, t), U32)
    fspec = pl.BlockSpec((SELECT_HEADS, PEER_KEYS, TS), lambda i, h: (h, 0, i))
    pspec = pl.BlockSpec((SELECT_HEADS, PEER_KEYS // 2, TS), lambda i, h: (h, 0, i))
    return pl.pallas_call(
        _peer_select_kernel,
        grid=(t // TS, PEER_HEADS // SELECT_HEADS),
        in_specs=[
            pl.BlockSpec((TS, D_MODEL), lambda i, h: (i, 0)),
            pl.BlockSpec((D_MODEL, SELECT_HEADS * 2 * PEER_HALF), lambda i, h: (0, h)),
            pl.BlockSpec((SELECT_HEADS, 2, PEER_KEYS, PEER_HALF), lambda i, h: (h, 0, 0, 0)),
        ],
        out_specs=[fspec, fspec, pspec, pspec],
        out_shape=[full, full, packed, packed],
        compiler_params=_cparams(("arbitrary", "arbitrary")),
        name="peer_select",
    )(x2, wq, sk)


def _peer_dense_kernel(x_ref, cp_ref, ap_ref, rp_ref, bp_ref, u_ref, vt_ref, vt_prev_ref, g_ref, bb_ref,
                       p_ref, wg_ref, wp_ref, y_ref, *scratch):
    e = pl.program_id(1)
    half = PEER_KEYS // 2
    xt_scr, acc_scr, c_first, c_second = scratch[:4]
    h_scrs = scratch[4:]
    rows_per_chunk = SUBLANES // DENSE_CHUNKS
    ec = rows_per_chunk * PEER_KEYS
    eh = ET // 2

    @pl.when(e == 0)
    def _():
        xt_scr[...] = x_ref[...].T.astype(BF16)
        acc_scr[...] = jnp.zeros(acc_scr.shape, F32)
        c_second[...] = jnp.zeros(c_second.shape, BF16)

    irows = pl.ds(pl.multiple_of(e * SUBLANES, SUBLANES), SUBLANES)

    def gelu(hv):
        return 0.5 * hv * (1.0 + lax.erf(hv * INV_SQRT2))

    def scores(ch):
        h_scrs[ch][...] = jnp.dot(u_ref[ch * ec:(ch + 1) * ec, :], xt_scr[...], preferred_element_type=F32)

    pieces = DENSE_CHUNKS // 2

    def accumulate(vt_cols, c_scr, part):
        rows = slice(part * (D_MODEL // pieces), (part + 1) * (D_MODEL // pieces))
        acc_scr[rows, :] += jnp.dot(vt_cols[rows, :], c_scr[...], preferred_element_type=F32)

    def coefficients(ch, c_scr):
        h_scr = h_scrs[ch]
        for tb in range(TT // LANES):
            sl = slice(tb * LANES, (tb + 1) * LANES)
            ws = [jnp.zeros((PEER_KEYS, LANES), BF16) for _ in range(rows_per_chunk)]
            for hd in range(PEER_HEADS):
                rank1 = pltpu.bitcast(rp_ref[hd, :, sl], BF16)
                b_fac = pltpu.bitcast(bp_ref[hd, :, sl], BF16)
                cnt8 = cp_ref[hd, irows, sl]
                a8 = ap_ref[hd, irows, sl]
                for ir in range(rows_per_chunk):
                    il = ch * rows_per_chunk + ir
                    cnt_row = jnp.broadcast_to(cnt8[il:il + 1, :], (half, LANES))
                    a_row = jnp.broadcast_to(a8[il:il + 1, :], (half, LANES))
                    picked = jnp.where(rank1 < pltpu.bitcast(cnt_row, BF16), b_fac, jnp.zeros_like(b_fac))
                    ws[ir] = ws[ir] + picked * pltpu.bitcast(a_row, BF16)
            for ir in range(rows_per_chunk):
                il = ch * rows_per_chunk + ir
                wu = pltpu.bitcast(ws[ir], U32)
                w_lo = pltpu.bitcast(wu << 16, F32)
                w_hi = pltpu.bitcast(wu & jnp.uint32(0xFFFF0000), F32)
                r0 = (il * PEER_KEYS) % eh
                h0 = ir * PEER_KEYS
                c_scr[r0:r0 + half, sl] = (w_lo * gelu(h_scr[h0:h0 + half, sl])).astype(BF16)
                c_scr[r0 + half:r0 + PEER_KEYS, sl] = (
                    w_hi * gelu(h_scr[h0 + half:h0 + PEER_KEYS, sl])).astype(BF16)

    scores(0)
    for ch in range(DENSE_CHUNKS):
        if ch + 1 < DENSE_CHUNKS:
            scores(ch + 1)
        if ch < pieces:
            accumulate(vt_prev_ref, c_second, ch)
            coefficients(ch, c_first)
        else:
            accumulate(vt_ref.at[:, :eh], c_first, ch - pieces)
            coefficients(ch, c_second)

    @pl.when(e == pl.num_programs(1) - 1)
    def _():
        acc_scr[...] += jnp.dot(vt_ref[:, eh:], c_second[...], preferred_element_type=F32)
        f = acc_scr[...].T
        y = _layer_norm(ALPHA * x_ref[...] + f, g_ref[...], bb_ref[...])
        gate = jax.nn.sigmoid(jnp.dot(y.astype(BF16), wg_ref[...], preferred_element_type=F32))
        proj = jnp.dot(p_ref[...].astype(BF16), wp_ref[...], preferred_element_type=F32)
        y_ref[...] = y + gate * proj


def _peer_dense(x2, sel, u_all, vt_all, g, b, p_all, wg, wp, layer):
    t = x2.shape[0]
    row = pl.BlockSpec((TT, D_MODEL), lambda i, e: (i, 0))
    vec = pl.BlockSpec((1, D_MODEL), lambda i, e: (0, 0))
    full = pl.BlockSpec((PEER_HEADS, PEER_KEYS, TT), lambda i, e: (0, 0, i))
    packed = pl.BlockSpec((PEER_HEADS, PEER_KEYS // 2, TT), lambda i, e: (0, 0, i))
    return pl.pallas_call(
        _peer_dense_kernel,
        grid=(t // TT, PEER_EXPERTS // ET),
        in_specs=[
            row, full, full, packed, packed,
            pl.BlockSpec((None, ET, D_MODEL), lambda i, e: (layer, e, 0)),
            pl.BlockSpec((None, D_MODEL, ET), lambda i, e: (layer, 0, e)),
            pl.BlockSpec((None, D_MODEL, ET // 2), lambda i, e: (layer, 0, jnp.maximum(2 * e - 1, 0))),
            vec, vec,
            pl.BlockSpec((None, TT, PLE_DIM), lambda i, e: (layer, i, 0)),
            pl.BlockSpec((D_MODEL, D_MODEL), lambda i, e: (0, 0)),
            pl.BlockSpec((PLE_DIM, D_MODEL), lambda i, e: (0, 0)),
        ],
        out_specs=row,
        out_shape=jax.ShapeDtypeStruct((t, D_MODEL), F32),
        scratch_shapes=(
            [pltpu.VMEM((D_MODEL, TT), BF16),
             pltpu.VMEM((D_MODEL, TT), F32),
             pltpu.VMEM((ET // 2, TT), BF16),
             pltpu.VMEM((ET // 2, TT), BF16)]
            + [pltpu.VMEM((ET // DENSE_CHUNKS, TT), F32)] * DENSE_CHUNKS
        ),
        compiler_params=_cparams(("arbitrary", "arbitrary")),
        name="peer_dense",
    )(x2, *sel, u_all, vt_all, vt_all, g, b, p_all, wg, wp)


def _rope_tables(dil):
    inv_freq = ROPE_THETA ** (-jnp.arange(0, ROT_DIM, 2, dtype=F32) / ROT_DIM)
    ang = jnp.arange(SEQ, dtype=F32)[:, None] * inv_freq[None, :]
    cos, sin = jnp.cos(ang), jnp.sin(ang)
    half = ROT_DIM // 2
    pad = HEAD_DIM - ROT_DIM
    c_tab = jnp.concatenate([cos, cos, jnp.ones((SEQ, pad), F32)], axis=1)
    sa_tab = jnp.concatenate([-sin, jnp.zeros((SEQ, half + pad), F32)], axis=1)
    sb_tab = jnp.concatenate([jnp.zeros((SEQ, half), F32), sin, jnp.zeros((SEQ, pad), F32)], axis=1)

    def residue_major(tab):
        return tab.reshape(SEQ // dil, dil, HEAD_DIM).transpose(1, 0, 2).reshape(SEQ, HEAD_DIM)

    return tuple(residue_major(tab) for tab in (c_tab, sa_tab, sb_tab))


def _trunk(x, p, attn_w_qkv, attn_w_out, conv_w_in, conv_kernel, conv_w_out,
           peer_w_query, peer_sub_keys, peer_u, peer_vt,
           ln1_gain, ln1_bias, ln2_gain, ln2_bias, ple_w_proj, ple_w_gate):
    nbatch = x.shape[0]
    t = nbatch * SEQ
    x2 = x.reshape(t, D_MODEL)
    tabs = [_rope_tables(dil) for dil in DILATIONS]
    p_all = p.reshape(DEPTH, t, PLE_DIM)
    for i in range(DEPTH):
        j = i // N_MIXERS
        g1 = ln1_gain[i].reshape(1, D_MODEL)
        b1 = ln1_bias[i].reshape(1, D_MODEL)
        if i % N_MIXERS == 0:
            x3 = x2.reshape(nbatch, SEQ, D_MODEL)
            qkvs = [_qkv_proj(x3, attn_w_qkv, j, g, tabs[g], dil) for g, dil in enumerate(DILATIONS)]
            o = _attention(qkvs)
            x2 = _proj_ln(x2, o.reshape(t, D_MODEL), attn_w_out[j], g1, b1)
        else:
            x2 = _conv_mixer_ln(x2, conv_w_in[j], conv_kernel[j], conv_w_out[j], g1, b1)
        sel = _peer_select(x2, peer_w_query[i], peer_sub_keys[i])
        x2 = _peer_dense(
            x2, sel, peer_u, peer_vt,
            ln2_gain[i].reshape(1, D_MODEL), ln2_bias[i].reshape(1, D_MODEL),
            p_all, ple_w_gate[i], ple_w_proj[i], i)
    return x2.reshape(nbatch, SEQ, D_MODEL)


def kernel(x_prompt, x_sample, p_prompt, p_sample, attn_w_qkv, attn_w_out, conv_w_in, conv_kernel, conv_w_out, peer_w_query, peer_sub_keys, peer_u, peer_v, ln1_gain, ln1_bias, ln2_gain, ln2_bias, ple_w_proj, ple_w_gate):
    weights = (attn_w_qkv.astype(BF16), attn_w_out.astype(BF16), conv_w_in.astype(BF16), conv_kernel,
               conv_w_out.astype(BF16), peer_w_query.astype(BF16), peer_sub_keys.astype(BF16),
               peer_u.astype(BF16), peer_v.astype(BF16).transpose(0, 2, 1),
               ln1_gain, ln1_bias, ln2_gain, ln2_bias, ple_w_proj.astype(BF16), ple_w_gate.astype(BF16))
    return (_trunk(x_prompt, p_prompt, *weights), _trunk(x_sample, p_sample, *weights))
```

```python
import functools
import math

import jax
import jax.numpy as jnp
from jax import lax
from jax.experimental import pallas as pl
from jax.experimental.pallas import tpu as pltpu

F32 = jnp.float32
BF16 = jnp.bfloat16
U32 = jnp.uint32

D_MODEL = 1024
SEQ = 2048
DEPTH = 4
N_MIXERS = 2
HEAD_DIM = 128
N_HEADS = D_MODEL // HEAD_DIM
DILATIONS = (1, 4, 16)
N_GROUPS = 3
GROUP_COLS = 3 * N_HEADS * HEAD_DIM
ROPE_THETA = 500000.0
ROT_DIM = HEAD_DIM // 4
NEG_INF = -1e30
PEER_HEADS = 8
PEER_KEYS = 128
PEER_EXPERTS = PEER_KEYS * PEER_KEYS
PEER_HALF = 128
PEER_TOPK = 16
PLE_DIM = 256
ALPHA = (2.0 * DEPTH) ** 0.25
LN_EPS = 1e-5
INV_SQRT2 = 0.7071067811865476

VMEM_LIMIT_BYTES = 56 * 1024 * 1024
LANES = 128
SUBLANES = 8

TM = 512
QB = 128
HALF_W = 64
TS = 512
SELECT_HEADS = 2
TT = 512
DENSE_CHUNKS = 4
ET = PEER_KEYS * SUBLANES
HALO = 8


def _cparams(sem, **flags):
    return pltpu.CompilerParams(dimension_semantics=sem, vmem_limit_bytes=VMEM_LIMIT_BYTES,
                                flags=flags or None)


def _layer_norm(z, g, b):
    mu = jnp.mean(z, axis=-1, keepdims=True)
    zc = z - mu
    var = jnp.mean(zc * zc, axis=-1, keepdims=True)
    return zc * lax.rsqrt(var + LN_EPS) * g + b


def _qkv_kernel(x_ref, w_ref, c_ref, sa_ref, sb_ref, o_ref, xs_scr, stage_scr, *, dil):
    rows = TM // dil
    if dil == 1:
        xs_scr[...] = x_ref[...].astype(BF16)
    else:
        nblk = D_MODEL // LANES
        for cb in range(nblk):
            stage_scr[cb] = x_ref[:, cb * LANES:(cb + 1) * LANES]
        for r in range(dil):
            for cb in range(nblk):
                xs_scr[r * rows:(r + 1) * rows, cb * LANES:(cb + 1) * LANES] = (
                    stage_scr.at[cb][pl.ds(r, rows, stride=dil), :].astype(BF16))
    c, sa, sb = (t_ref[...].reshape(TM, HEAD_DIM) for t_ref in (c_ref, sa_ref, sb_ref))
    qk_cols = 2 * N_HEADS * HEAD_DIM

    def emit(cols, val):
        for r in range(dil):
            o_ref[r, :, cols] = val[r * rows:(r + 1) * rows, :].astype(o_ref.dtype)

    for pair in range(N_HEADS):
        t2 = jnp.dot(xs_scr[...], w_ref[:, pair * 2 * HEAD_DIM:(pair + 1) * 2 * HEAD_DIM],
                     preferred_element_type=F32)
        for hb in range(2):
            t = t2[:, hb * HEAD_DIM:(hb + 1) * HEAD_DIM]
            t = (t * c + pltpu.roll(t, LANES - ROT_DIM // 2, axis=1) * sa
                 + pltpu.roll(t, ROT_DIM // 2, axis=1) * sb)
            emit(slice((2 * pair + hb) * HEAD_DIM, (2 * pair + hb + 1) * HEAD_DIM), t)
    emit(slice(qk_cols, GROUP_COLS), jnp.dot(xs_scr[...], w_ref[:, qk_cols:], preferred_element_type=F32))


def _qkv_proj(x3, w_all, layer, group, tabs, dil):
    nbatch = x3.shape[0]
    sub_len = SEQ // dil
    rows = TM // dil
    tab = pl.BlockSpec((dil, rows, HEAD_DIM), lambda b, s: (0, s, 0))
    out = pl.pallas_call(
        functools.partial(_qkv_kernel, dil=dil),
        grid=(nbatch, SEQ // TM),
        in_specs=[
            pl.BlockSpec((None, TM, D_MODEL), lambda b, s: (b, s, 0)),
            pl.BlockSpec((None, D_MODEL, GROUP_COLS), lambda b, s: (layer, 0, group)),
            tab, tab, tab,
        ],
        out_specs=pl.BlockSpec((None, dil, rows, GROUP_COLS), lambda b, s: (b, 0, s, 0)),
        out_shape=jax.ShapeDtypeStruct((nbatch, dil, sub_len, GROUP_COLS), BF16),
        scratch_shapes=[pltpu.VMEM((TM, D_MODEL), BF16),
                        pltpu.VMEM((D_MODEL // LANES, TM, LANES), F32)],
        compiler_params=_cparams(("arbitrary", "arbitrary")),
        name=f"qkv_proj_d{dil}",
    )(x3, w_all, *(tab_.reshape(dil, sub_len, HEAD_DIM) for tab_ in tabs))
    return out.reshape(nbatch, SEQ, GROUP_COLS)


def _attn_kernel(q0, k0, v0, q1, k1, v1, q2, k2, v2, o_ref, bias_scr, acc_scr, m_scr, l_scr):
    scale = 1.0 / math.sqrt(HEAD_DIM)
    kw_full = QB + 2 * HALF_W
    for case in range(3):
        qi = case * HALF_W + lax.broadcasted_iota(jnp.int32, (QB, kw_full), 0)
        ki = lax.broadcasted_iota(jnp.int32, (QB, kw_full), 1)
        bias_scr[case] = jnp.where(jnp.abs(qi - ki) <= HALF_W, 0.0, NEG_INF)

    nt = (((1,), (1,)), ((), ()))
    for g, ((q_ref, k_ref, v_ref), dil) in enumerate(zip(((q0, k0, v0), (q1, k1, v1), (q2, k2, v2)), DILATIONS)):
        sub_len = SEQ // dil
        nb = sub_len // QB
        kw = min(kw_full, sub_len)
        for bi in range(dil * nb):
            r, mb = divmod(bi, nb)
            m0 = mb * QB
            ws = min(max(m0 - HALF_W, 0), sub_len - kw)
            q = q_ref[bi * QB:(bi + 1) * QB, :]
            k = k_ref[r * sub_len + ws:r * sub_len + ws + kw, :]
            v = v_ref[r * sub_len + ws:r * sub_len + ws + kw, :]
            s = lax.dot_general(q, k, nt, preferred_element_type=F32) * scale
            s = s + bias_scr[(m0 - ws) // HALF_W][:, :kw]
            m = jnp.max(s, axis=-1, keepdims=True)
            p = jnp.exp(s - m)
            den = jnp.sum(p, axis=-1, keepdims=True)
            num = jnp.dot(p.astype(BF16), v, preferred_element_type=F32)
            rows = pl.ds(r + dil * m0, QB, stride=dil) if dil > 1 else pl.ds(m0, QB)
            acc_scr.at[g][rows, :] = num
            m_scr.at[g][rows, :] = jnp.broadcast_to(m, (QB, HEAD_DIM))
            l_scr.at[g][rows, :] = jnp.broadcast_to(den, (QB, HEAD_DIM))

    m_all = jnp.maximum(jnp.maximum(m_scr[0], m_scr[1]), m_scr[2])
    num = jnp.zeros((SEQ, HEAD_DIM), F32)
    den = jnp.zeros((SEQ, HEAD_DIM), F32)
    for g in range(N_GROUPS):
        w = jnp.exp(m_scr[g] - m_all)
        num = num + w * acc_scr[g]
        den = den + w * l_scr[g]
    o_ref[...] = (num / den).astype(o_ref.dtype)


def _attention(qkvs):
    nbatch = qkvs[0].shape[0]

    def col(part):
        return lambda b, h: (b, 0, part * N_HEADS + h)

    in_specs = [pl.BlockSpec((None, SEQ, HEAD_DIM), col(part)) for _ in range(N_GROUPS) for part in range(3)]
    operands = [qkvs[g] for g in range(N_GROUPS) for _ in range(3)]
    return pl.pallas_call(
        _attn_kernel,
        grid=(nbatch, N_HEADS),
        in_specs=in_specs,
        out_specs=pl.BlockSpec((None, SEQ, HEAD_DIM), lambda b, h: (b, 0, h)),
        out_shape=jax.ShapeDtypeStruct((nbatch, SEQ, D_MODEL), BF16),
        scratch_shapes=[
            pltpu.VMEM((3, QB, QB + 2 * HALF_W), F32),
            pltpu.VMEM((N_GROUPS, SEQ, HEAD_DIM), F32),
            pltpu.VMEM((N_GROUPS, SEQ, HEAD_DIM), F32),
            pltpu.VMEM((N_GROUPS, SEQ, HEAD_DIM), F32),
        ],
        compiler_params=_cparams(("arbitrary", "arbitrary")),
        name="dilated_attention",
    )(*operands)


def _proj_ln_kernel(x_ref, o_ref, w_ref, g_ref, b_ref, y_ref):
    h = jnp.dot(o_ref[...], w_ref[...], preferred_element_type=F32)
    y_ref[...] = _layer_norm(ALPHA * x_ref[...] + h, g_ref[...], b_ref[...])


def _proj_ln(x2, o2, w, g, b):
    t = x2.shape[0]
    row = pl.BlockSpec((TM, D_MODEL), lambda i: (i, 0))
    vec = pl.BlockSpec((1, D_MODEL), lambda i: (0, 0))
    return pl.pallas_call(
        _proj_ln_kernel,
        grid=(t // TM,),
        in_specs=[row, row, pl.BlockSpec((D_MODEL, D_MODEL), lambda i: (0, 0)), vec, vec],
        out_specs=row,
        out_shape=jax.ShapeDtypeStruct((t, D_MODEL), F32),
        compiler_params=_cparams(("arbitrary",)),
        name="attn_out_ln",
    )(x2, o2, w, g, b)


def _conv_kernel(xp_ref, x_ref, xn_ref, win_ref, ck_ref, wout_ref, g_ref, b_ref, y_ref, xcat, u_scr):
    i = pl.program_id(0)
    xcat[0:HALO, :] = xp_ref[0].astype(BF16)
    xcat[HALO:HALO + TM, :] = x_ref[...].astype(BF16)
    xcat[HALO + TM:, :] = xn_ref[0].astype(BF16)
    u_scr[...] = jnp.dot(xcat[...], win_ref[...], preferred_element_type=F32)
    hh = u_scr[:, D_MODEL:2 * D_MODEL] * u_scr[:, 2 * D_MODEL:]
    pos = (i * TM + lax.broadcasted_iota(jnp.int32, (TM, 1), 0)) % SEQ
    prev = jnp.where(pos == 0, 0.0, hh[HALO - 1:HALO - 1 + TM, :])
    nxt = jnp.where(pos == SEQ - 1, 0.0, hh[HALO + 1:HALO + 1 + TM, :])
    ck = ck_ref[...]
    conv = prev * ck[0:1, :] + hh[HALO:HALO + TM, :] * ck[1:2, :] + nxt * ck[2:3, :]
    gated = (u_scr[HALO:HALO + TM, 0:D_MODEL] * conv).astype(BF16)
    h = jnp.dot(gated, wout_ref[...], preferred_element_type=F32)
    y_ref[...] = _layer_norm(ALPHA * x_ref[...] + h, g_ref[...], b_ref[...])


def _conv_mixer_ln(x2, w_in, ck, w_out, g, b):
    t = x2.shape[0]
    nblk = t // HALO
    x3 = x2.reshape(nblk, HALO, D_MODEL)
    per = TM // HALO
    row = pl.BlockSpec((TM, D_MODEL), lambda i: (i, 0))
    vec = pl.BlockSpec((1, D_MODEL), lambda i: (0, 0))
    return pl.pallas_call(
        _conv_kernel,
        grid=(t // TM,),
        in_specs=[
            pl.BlockSpec((1, HALO, D_MODEL), lambda i: (jnp.maximum(i * per - 1, 0), 0, 0)),
            row,
            pl.BlockSpec((1, HALO, D_MODEL), lambda i: (jnp.minimum((i + 1) * per, nblk - 1), 0, 0)),
            pl.BlockSpec((D_MODEL, 3 * D_MODEL), lambda i: (0, 0)),
            pl.BlockSpec((3, D_MODEL), lambda i: (0, 0)),
            pl.BlockSpec((D_MODEL, D_MODEL), lambda i: (0, 0)),
            vec, vec,
        ],
        out_specs=row,
        out_shape=jax.ShapeDtypeStruct((t, D_MODEL), F32),
        scratch_shapes=[
            pltpu.VMEM((TM + 2 * HALO, D_MODEL), BF16),
            pltpu.VMEM((TM + 2 * HALO, 3 * D_MODEL), F32),
        ],
        compiler_params=_cparams(("arbitrary",)),
        name="conv_mixer_ln",
    )(x3, x2, x3, w_in, ck, w_out, g, b)


def _batcher_network(n):
    def merge(lo, hi, r):
        step = r * 2
        if step < hi - lo:
            yield from merge(lo, hi, step)
            yield from merge(lo + r, hi, step)
            yield from ((i, i + r) for i in range(lo + r, hi - r, step))
        else:
            yield (lo, lo + r)

    def sort(lo, hi):
        if hi - lo >= 1:
            mid = lo + (hi - lo) // 2
            yield from sort(lo, mid)
            yield from sort(mid + 1, hi)
            yield from merge(lo, hi, 1)

    return tuple(sort(0, n - 1))


_SORT16 = _batcher_network(PEER_TOPK)
_BITONIC16 = tuple((i, i + d) for d in (8, 4, 2, 1) for i in range(PEER_TOPK) if not i & d)
_CAND_ROW_LIMIT = tuple(PEER_TOPK // (a + 1) for a in range(PEER_TOPK))


def _compare_exchange(v, pairs):
    for i, j in pairs:
        v[i], v[j] = jnp.maximum(v[i], v[j]), jnp.minimum(v[i], v[j])


def _top16_sorted(rows):
    v = list(rows)
    _compare_exchange(v, _SORT16)
    for shift in (4, 2, 1):
        other = [pltpu.roll(x, shift, axis=0) for x in v]
        v = [jnp.maximum(v[k], other[PEER_TOPK - 1 - k]) for k in range(PEER_TOPK)]
        _compare_exchange(v, _BITONIC16)
    return v


def _dup16(x):
    hi = pltpu.bitcast(x, U32) & jnp.uint32(0xFFFF0000)
    return hi | (hi >> 16)


def _pack2(lo, hi):
    lo = pltpu.bitcast(lo.astype(BF16).astype(F32), U32)
    hi = pltpu.bitcast(hi.astype(BF16).astype(F32), U32)
    return (hi & jnp.uint32(0xFFFF0000)) | (lo >> 16)


def _select_block(s0, s1):
    nrow = len(s0)
    v0 = _top16_sorted(s0)
    v1 = _top16_sorted(s1)
    lanes = s0[0].shape[1]
    row = lax.broadcasted_iota(jnp.int32, (SUBLANES, lanes), 0)

    def rows_of(vals):
        out = vals[0]
        for r in range(1, SUBLANES):
            out = jnp.where(row == r, vals[r], out)
        return out

    v1_lo, v1_hi, v0_hi = rows_of(v1[:SUBLANES]), rows_of(v1[SUBLANES:]), rows_of(v0[SUBLANES:])
    cands = [v0[0] + v1_lo, v0[0] + v1_hi]
    for a in range(1, SUBLANES):
        cands.append(jnp.where(row < _CAND_ROW_LIMIT[a], v0[a] + v1_lo, -jnp.inf))
    cands.append(v0_hi + v1[0])
    work = list(cands)
    tau = None
    for k in range(PEER_TOPK):
        m = work[0]
        for w in work[1:]:
            m = jnp.maximum(m, w)
        tau = jnp.max(m, axis=0, keepdims=True)
        if k + 1 < PEER_TOPK:
            work = [jnp.where(w == tau, -jnp.inf, w) for w in work]
    top = v0[0] + v1[0]
    sel = [c >= tau for c in cands]
    z = jnp.zeros((1, lanes), F32)
    for c, m in zip(cands, sel):
        z = z + jnp.sum(jnp.where(m, jnp.exp(c - top), 0.0), axis=0, keepdims=True)
    ones = [jnp.where(m, 1.0, 0.0) for m in sel]
    counts = [jnp.sum(ones[0] + ones[1], axis=0, keepdims=True)]
    counts += [jnp.sum(ones[1 + a], axis=0, keepdims=True) for a in range(1, SUBLANES)]
    tau_b = jnp.broadcast_to(tau, (SUBLANES, lanes))
    cnt = [jnp.where(s0[k] < v0[SUBLANES - 1],
                     jnp.where(s0[k] >= v0[PEER_TOPK - 1], jnp.where(s0[k] + v1[0] >= tau_b, 1.0, 0.0), 0.0),
                     0.0) for k in range(nrow)]
    for a in range(SUBLANES):
        ca = jnp.broadcast_to(counts[a], (SUBLANES, lanes))
        cnt = [jnp.where(s0[k] == v0[a], ca, cnt[k]) for k in range(nrow)]
    zinv = jnp.broadcast_to(1.0 / z, (SUBLANES, lanes))
    a_fac = [jnp.exp(s0[k] - v0[0]) * zinv for k in range(nrow)]
    b_fac = [jnp.exp(s1[k] - v1[0]) for k in range(nrow)]
    rank1 = [jnp.zeros((SUBLANES, lanes), F32) for _ in range(nrow)]
    for b in range(PEER_TOPK):
        rank1 = [jnp.where(v1[b] > s1[k], float(b + 1), rank1[k]) for k in range(nrow)]
    return cnt, a_fac, rank1, b_fac


def _peer_select_kernel(x_ref, wq_ref, sk_ref, cp_ref, ap_ref, rp_ref, bp_ref):
    xb = x_ref[...].astype(BF16)
    nt = (((1,), (1,)), ((), ()))
    nrow = PEER_KEYS // SUBLANES
    half = nrow // 2
    for hd in range(SELECT_HEADS):
        qcols = slice(hd * 2 * PEER_HALF, (hd + 1) * 2 * PEER_HALF)
        q = jnp.dot(xb, wq_ref[:, qcols], preferred_element_type=F32).astype(BF16)
        s0 = lax.dot_general(sk_ref[hd, 0], q[:, :PEER_HALF], nt, preferred_element_type=F32)
        s1 = lax.dot_general(sk_ref[hd, 1], q[:, PEER_HALF:], nt, preferred_element_type=F32)
        for tb in range(TS // LANES):
            sl = slice(tb * LANES, (tb + 1) * LANES)
            r0 = [s0[SUBLANES * k:SUBLANES * (k + 1), sl] for k in range(nrow)]
            r1 = [s1[SUBLANES * k:SUBLANES * (k + 1), sl] for k in range(nrow)]
            cnt, a_fac, rank1, b_fac = _select_block(r0, r1)
            for k in range(nrow):
                rows = slice(SUBLANES * k, SUBLANES * (k + 1))
                cp_ref[hd, rows, sl] = _dup16(cnt[k])
                ap_ref[hd, rows, sl] = _dup16(a_fac[k].astype(BF16).astype(F32))
            for k in range(half):
                rows = slice(SUBLANES * k, SUBLANES * (k + 1))
                rp_ref[hd, rows, sl] = _pack2(rank1[k], rank1[k + half])
                bp_ref[hd, rows, sl] = _pack2(b_fac[k], b_fac[k + half])


def _peer_select(x2, wq, sk):
    t = x2.shape[0]
    full = jax.ShapeDtypeStruct((PEER_HEADS, PEER_KEYS, t), U32)
    packed = jax.ShapeDtypeStruct((PEER_HEADS, PEER_KEYS // 2, t), U32)
    fspec = pl.BlockSpec((SELECT_HEADS, PEER_KEYS, TS), lambda i, h: (h, 0, i))
    pspec = pl.BlockSpec((SELECT_HEADS, PEER_KEYS // 2, TS), lambda i, h: (h, 0, i))
    return pl.pallas_call(
        _peer_select_kernel,
        grid=(t // TS, PEER_HEADS // SELECT_HEADS),
        in_specs=[
            pl.BlockSpec((TS, D_MODEL), lambda i, h: (i, 0)),
            pl.BlockSpec((D_MODEL, SELECT_HEADS * 2 * PEER_HALF), lambda i, h: (0, h)),
            pl.BlockSpec((SELECT_HEADS, 2, PEER_KEYS, PEER_HALF), lambda i, h: (h, 0, 0, 0)),
        ],
        out_specs=[fspec, fspec, pspec, pspec],
        out_shape=[full, full, packed, packed],
        compiler_params=_cparams(("arbitrary", "arbitrary")),
        name="peer_select",
    )(x2, wq, sk)


def _peer_dense_kernel(x_ref, cp_ref, ap_ref, rp_ref, bp_ref, u_ref, vt_ref, vt_prev_ref, g_ref, bb_ref,
                       p_ref, wg_ref, wp_ref, y_ref, *scratch):
    e = pl.program_id(1)
    half = PEER_KEYS // 2
    xt_scr, acc_scr, c_first, c_second = scratch[:4]
    h_scrs = scratch[4:]
    rows_per_chunk = SUBLANES // DENSE_CHUNKS
    ec = rows_per_chunk * PEER_KEYS
    eh = ET // 2

    @pl.when(e == 0)
    def _():
        xt_scr[...] = x_ref[...].T.astype(BF16)
        acc_scr[...] = jnp.zeros(acc_scr.shape, F32)
        c_second[...] = jnp.zeros(c_second.shape, BF16)

    irows = pl.ds(pl.multiple_of(e * SUBLANES, SUBLANES), SUBLANES)

    def gelu(hv):
        return 0.5 * hv * (1.0 + lax.erf(hv * INV_SQRT2))

    def scores(ch):
        h_scrs[ch][...] = jnp.dot(u_ref[ch * ec:(ch + 1) * ec, :], xt_scr[...], preferred_element_type=F32)

    pieces = DENSE_CHUNKS // 2

    def accumulate(vt_cols, c_scr, part):
        rows = slice(part * (D_MODEL // pieces), (part + 1) * (D_MODEL // pieces))
        acc_scr[rows, :] += jnp.dot(vt_cols[rows, :], c_scr[...], preferred_element_type=F32)

    def coefficients(ch, c_scr):
        h_scr = h_scrs[ch]
        for tb in range(TT // LANES):
            sl = slice(tb * LANES, (tb + 1) * LANES)
            ws = [jnp.zeros((PEER_KEYS, LANES), BF16) for _ in range(rows_per_chunk)]
            for hd in range(PEER_HEADS):
                rank1 = pltpu.bitcast(rp_ref[hd, :, sl], BF16)
                b_fac = pltpu.bitcast(bp_ref[hd, :, sl], BF16)
                cnt8 = cp_ref[hd, irows, sl]
                a8 = ap_ref[hd, irows, sl]
                for ir in range(rows_per_chunk):
                    il = ch * rows_per_chunk + ir
                    cnt_row = jnp.broadcast_to(cnt8[il:il + 1, :], (half, LANES))
                    a_row = jnp.broadcast_to(a8[il:il + 1, :], (half, LANES))
                    picked = jnp.where(rank1 < pltpu.bitcast(cnt_row, BF16), b_fac, jnp.zeros_like(b_fac))
                    ws[ir] = ws[ir] + picked * pltpu.bitcast(a_row, BF16)
            for ir in range(rows_per_chunk):
                il = ch * rows_per_chunk + ir
                wu = pltpu.bitcast(ws[ir], U32)
                w_lo = pltpu.bitcast(wu << 16, F32)
                w_hi = pltpu.bitcast(wu & jnp.uint32(0xFFFF0000), F32)
                r0 = (il * PEER_KEYS) % eh
                h0 = ir * PEER_KEYS
                c_scr[r0:r0 + half, sl] = (w_lo * gelu(h_scr[h0:h0 + half, sl])).astype(BF16)
                c_scr[r0 + half:r0 + PEER_KEYS, sl] = (
                    w_hi * gelu(h_scr[h0 + half:h0 + PEER_KEYS, sl])).astype(BF16)

    scores(0)
    for ch in range(DENSE_CHUNKS):
        if ch + 1 < DENSE_CHUNKS:
            scores(ch + 1)
        if ch < pieces:
            accumulate(vt_prev_ref, c_second, ch)
            coefficients(ch, c_first)
        else:
            accumulate(vt_ref.at[:, :eh], c_first, ch - pieces)
            coefficients(ch, c_second)

    @pl.when(e == pl.num_programs(1) - 1)
    def _():
        acc_scr[...] += jnp.dot(vt_ref[:, eh:], c_second[...], preferred_element_type=F32)
        f = acc_scr[...].T
        y = _layer_norm(ALPHA * x_ref[...] + f, g_ref[...], bb_ref[...])
        gate = jax.nn.sigmoid(jnp.dot(y.astype(BF16), wg_ref[...], preferred_element_type=F32))
        proj = jnp.dot(p_ref[...].astype(BF16), wp_ref[...], preferred_element_type=F32)
        y_ref[...] = y + gate * proj


def _peer_dense(x2, sel, u_all, vt_all, g, b, p_all, wg, wp, layer):
    t = x2.shape[0]
    row = pl.BlockSpec((TT, D_MODEL), lambda i, e: (i, 0))
    vec = pl.BlockSpec((1, D_MODEL), lambda i, e: (0, 0))
    full = pl.BlockSpec((PEER_HEADS, PEER_KEYS, TT), lambda i, e: (0, 0, i))
    packed = pl.BlockSpec((PEER_HEADS, PEER_KEYS // 2, TT), lambda i, e: (0, 0, i))
    return pl.pallas_call(
        _peer_dense_kernel,
        grid=(t // TT, PEER_EXPERTS // ET),
        in_specs=[
            row, full, full, packed, packed,
            pl.BlockSpec((None, ET, D_MODEL), lambda i, e: (layer, e, 0)),
            pl.BlockSpec((None, D_MODEL, ET), lambda i, e: (layer, 0, e)),
            pl.BlockSpec((None, D_MODEL, ET // 2), lambda i, e: (layer, 0, jnp.maximum(2 * e - 1, 0))),
            vec, vec,
            pl.BlockSpec((None, TT, PLE_DIM), lambda i, e: (layer, i, 0)),
            pl.BlockSpec((D_MODEL, D_MODEL), lambda i, e: (0, 0)),
            pl.BlockSpec((PLE_DIM, D_MODEL), lambda i, e: (0, 0)),
        ],
        out_specs=row,
        out_shape=jax.ShapeDtypeStruct((t, D_MODEL), F32),
        scratch_shapes=(
            [pltpu.VMEM((D_MODEL, TT), BF16),
             pltpu.VMEM((D_MODEL, TT), F32),
             pltpu.VMEM((ET // 2, TT), BF16),
             pltpu.VMEM((ET // 2, TT), BF16)]
            + [pltpu.VMEM((ET // DENSE_CHUNKS, TT), F32)] * DENSE_CHUNKS
        ),
        compiler_params=_cparams(("arbitrary", "arbitrary")),
        name="peer_dense",
    )(x2, *sel, u_all, vt_all, vt_all, g, b, p_all, wg, wp)


def _rope_tables(dil):
    inv_freq = ROPE_THETA ** (-jnp.arange(0, ROT_DIM, 2, dtype=F32) / ROT_DIM)
    ang = jnp.arange(SEQ, dtype=F32)[:, None] * inv_freq[None, :]
    cos, sin = jnp.cos(ang), jnp.sin(ang)
    half = ROT_DIM // 2
    pad = HEAD_DIM - ROT_DIM
    c_tab = jnp.concatenate([cos, cos, jnp.ones((SEQ, pad), F32)], axis=1)
    sa_tab = jnp.concatenate([-sin, jnp.zeros((SEQ, half + pad), F32)], axis=1)
    sb_tab = jnp.concatenate([jnp.zeros((SEQ, half), F32), sin, jnp.zeros((SEQ, pad), F32)], axis=1)

    def residue_major(tab):
        return tab.reshape(SEQ // dil, dil, HEAD_DIM).transpose(1, 0, 2).reshape(SEQ, HEAD_DIM)

    return tuple(residue_major(tab) for tab in (c_tab, sa_tab, sb_tab))


def _trunk(x, p, attn_w_qkv, attn_w_out, conv_w_in, conv_kernel, conv_w_out,
           peer_w_query, peer_sub_keys, peer_u, peer_vt,
           ln1_gain, ln1_bias, ln2_gain, ln2_bias, ple_w_proj, ple_w_gate):
    nbatch = x.shape[0]
    t = nbatch * SEQ
    x2 = x.reshape(t, D_MODEL)
    tabs = [_rope_tables(dil) for dil in DILATIONS]
    p_all = p.reshape(DEPTH, t, PLE_DIM)
    for i in range(DEPTH):
        j = i // N_MIXERS
        g1 = ln1_gain[i].reshape(1, D_MODEL)
        b1 = ln1_bias[i].reshape(1, D_MODEL)
        if i % N_MIXERS == 0:
            x3 = x2.reshape(nbatch, SEQ, D_MODEL)
            qkvs = [_qkv_proj(x3, attn_w_qkv, j, g, tabs[g], dil) for g, dil in enumerate(DILATIONS)]
            o = _attention(qkvs)
            x2 = _proj_ln(x2, o.reshape(t, D_MODEL), attn_w_out[j], g1, b1)
        else:
            x2 = _conv_mixer_ln(x2, conv_w_in[j], conv_kernel[j], conv_w_out[j], g1, b1)
        sel = _peer_select(x2, peer_w_query[i], peer_sub_keys[i])
        x2 = _peer_dense(
            x2, sel, peer_u, peer_vt,
            ln2_gain[i].reshape(1, D_MODEL), ln2_bias[i].reshape(1, D_MODEL),
            p_all, ple_w_gate[i], ple_w_proj[i], i)
    return x2.reshape(nbatch, SEQ, D_MODEL)


def kernel(x_prompt, x_sample, p_prompt, p_sample, attn_w_qkv, attn_w_out, conv_w_in, conv_kernel, conv_w_out, peer_w_query, peer_sub_keys, peer_u, peer_v, ln1_gain, ln1_bias, ln2_gain, ln2_bias, ple_w_proj, ple_w_gate):
    weights = (attn_w_qkv.astype(BF16), attn_w_out.astype(BF16), conv_w_in.astype(BF16), conv_kernel,
               conv_w_out.astype(BF16), peer_w_query.astype(BF16), peer_sub_keys.astype(BF16),
               peer_u.astype(BF16), peer_v.astype(BF16).transpose(0, 2, 1),
               ln1_gain, ln1_bias, ln2_gain, ln2_bias, ple_w_proj.astype(BF16), ple_w_gate.astype(BF16))
    return (_trunk(x_prompt, p_prompt, *weights), _trunk(x_sample, p_sample, *weights))
```

```python
import functools
import math

import jax
import jax.numpy as jnp
from jax import lax
from jax.experimental import pallas as pl
from jax.experimental.pallas import tpu as pltpu

F32 = jnp.float32
BF16 = jnp.bfloat16
U32 = jnp.uint32

D_MODEL = 1024
SEQ = 2048
DEPTH = 4
N_MIXERS = 2
HEAD_DIM = 128
N_HEADS = D_MODEL // HEAD_DIM
DILATIONS = (1, 4, 16)
N_GROUPS = 3
GROUP_COLS = 3 * N_HEADS * HEAD_DIM
ROPE_THETA = 500000.0
ROT_DIM = HEAD_DIM // 4
NEG_INF = -1e30
PEER_HEADS = 8
PEER_KEYS = 128
PEER_EXPERTS = PEER_KEYS * PEER_KEYS
PEER_HALF = 128
PEER_TOPK = 16
PLE_DIM = 256
ALPHA = (2.0 * DEPTH) ** 0.25
LN_EPS = 1e-5
INV_SQRT2 = 0.7071067811865476

VMEM_LIMIT_BYTES = 56 * 1024 * 1024
LANES = 128
SUBLANES = 8

TM = 512
QB = 128
HALF_W = 64
TS = 512
SELECT_HEADS = 2
TT = 512
DENSE_CHUNKS = 4
ET = PEER_KEYS * SUBLANES
HALO = 8


def _cparams(sem, **flags):
    return pltpu.CompilerParams(dimension_semantics=sem, vmem_limit_bytes=VMEM_LIMIT_BYTES,
                                flags=flags or None)


def _layer_norm(z, g, b):
    mu = jnp.mean(z, axis=-1, keepdims=True)
    zc = z - mu
    var = jnp.mean(zc * zc, axis=-1, keepdims=True)
    return zc * lax.rsqrt(var + LN_EPS) * g + b


def _qkv_kernel(x_ref, w_ref, c_ref, sa_ref, sb_ref, o_ref, xs_scr, stage_scr, *, dil):
    rows = TM // dil
    if dil == 1:
        xs_scr[...] = x_ref[...].astype(BF16)
    else:
        nblk = D_MODEL // LANES
        for cb in range(nblk):
            stage_scr[cb] = x_ref[:, cb * LANES:(cb + 1) * LANES]
        for r in range(dil):
            for cb in range(nblk):
                xs_scr[r * rows:(r + 1) * rows, cb * LANES:(cb + 1) * LANES] = (
                    stage_scr.at[cb][pl.ds(r, rows, stride=dil), :].astype(BF16))
    c, sa, sb = (t_ref[...].reshape(TM, HEAD_DIM) for t_ref in (c_ref, sa_ref, sb_ref))
    qk_cols = 2 * N_HEADS * HEAD_DIM

    def emit(cols, val):
        for r in range(dil):
            o_ref[r, :, cols] = val[r * rows:(r + 1) * rows, :].astype(o_ref.dtype)

    for pair in range(N_HEADS):
        t2 = jnp.dot(xs_scr[...], w_ref[:, pair * 2 * HEAD_DIM:(pair + 1) * 2 * HEAD_DIM],
                     preferred_element_type=F32)
        for hb in range(2):
            t = t2[:, hb * HEAD_DIM:(hb + 1) * HEAD_DIM]
            t = (t * c + pltpu.roll(t, LANES - ROT_DIM // 2, axis=1) * sa
                 + pltpu.roll(t, ROT_DIM // 2, axis=1) * sb)
            emit(slice((2 * pair + hb) * HEAD_DIM, (2 * pair + hb + 1) * HEAD_DIM), t)
    emit(slice(qk_cols, GROUP_COLS), jnp.dot(xs_scr[...], w_ref[:, qk_cols:], preferred_element_type=F32))


def _qkv_proj(x3, w_all, layer, group, tabs, dil):
    nbatch = x3.shape[0]
    sub_len = SEQ // dil
    rows = TM // dil
    tab = pl.BlockSpec((dil, rows, HEAD_DIM), lambda b, s: (0, s, 0))
    out = pl.pallas_call(
        functools.partial(_qkv_kernel, dil=dil),
        grid=(nbatch, SEQ // TM),
        in_specs=[
            pl.BlockSpec((None, TM, D_MODEL), lambda b, s: (b, s, 0)),
            pl.BlockSpec((None, D_MODEL, GROUP_COLS), lambda b, s: (layer, 0, group)),
            tab, tab, tab,
        ],
        out_specs=pl.BlockSpec((None, dil, rows, GROUP_COLS), lambda b, s: (b, 0, s, 0)),
        out_shape=jax.ShapeDtypeStruct((nbatch, dil, sub_len, GROUP_COLS), BF16),
        scratch_shapes=[pltpu.VMEM((TM, D_MODEL), BF16),
                        pltpu.VMEM((D_MODEL // LANES, TM, LANES), F32)],
        compiler_params=_cparams(("arbitrary", "arbitrary")),
        name=f"qkv_proj_d{dil}",
    )(x3, w_all, *(tab_.reshape(dil, sub_len, HEAD_DIM) for tab_ in tabs))
    return out.reshape(nbatch, SEQ, GROUP_COLS)


def _attn_kernel(q0, k0, v0, q1, k1, v1, q2, k2, v2, o_ref, bias_scr, acc_scr, m_scr, l_scr):
    scale = 1.0 / math.sqrt(HEAD_DIM)
    kw_full = QB + 2 * HALF_W
    for case in range(3):
        qi = case * HALF_W + lax.broadcasted_iota(jnp.int32, (QB, kw_full), 0)
        ki = lax.broadcasted_iota(jnp.int32, (QB, kw_full), 1)
        bias_scr[case] = jnp.where(jnp.abs(qi - ki) <= HALF_W, 0.0, NEG_INF)

    nt = (((1,), (1,)), ((), ()))
    for g, ((q_ref, k_ref, v_ref), dil) in enumerate(zip(((q0, k0, v0), (q1, k1, v1), (q2, k2, v2)), DILATIONS)):
        sub_len = SEQ // dil
        nb = sub_len // QB
        kw = min(kw_full, sub_len)
        for bi in range(dil * nb):
            r, mb = divmod(bi, nb)
            m0 = mb * QB
            ws = min(max(m0 - HALF_W, 0), sub_len - kw)
            q = q_ref[bi * QB:(bi + 1) * QB, :]
            k = k_ref[r * sub_len + ws:r * sub_len + ws + kw, :]
            v = v_ref[r * sub_len + ws:r * sub_len + ws + kw, :]
            s = lax.dot_general(q, k, nt, preferred_element_type=F32) * scale
            s = s + bias_scr[(m0 - ws) // HALF_W][:, :kw]
            m = jnp.max(s, axis=-1, keepdims=True)
            p = jnp.exp(s - m)
            den = jnp.sum(p, axis=-1, keepdims=True)
            num = jnp.dot(p.astype(BF16), v, preferred_element_type=F32)
            rows = pl.ds(r + dil * m0, QB, stride=dil) if dil > 1 else pl.ds(m0, QB)
            acc_scr.at[g][rows, :] = num
            m_scr.at[g][rows, :] = jnp.broadcast_to(m, (QB, HEAD_DIM))
            l_scr.at[g][rows, :] = jnp.broadcast_to(den, (QB, HEAD_DIM))

    m_all = jnp.maximum(jnp.maximum(m_scr[0], m_scr[1]), m_scr[2])
    num = jnp.zeros((SEQ, HEAD_DIM), F32)
    den = jnp.zeros((SEQ, HEAD_DIM), F32)
    for g in range(N_GROUPS):
        w = jnp.exp(m_scr[g] - m_all)
        num = num + w * acc_scr[g]
        den = den + w * l_scr[g]
    o_ref[...] = (num / den).astype(o_ref.dtype)


def _attention(qkvs):
    nbatch = qkvs[0].shape[0]

    def col(part):
        return lambda b, h: (b, 0, part * N_HEADS + h)

    in_specs = [pl.BlockSpec((None, SEQ, HEAD_DIM), col(part)) for _ in range(N_GROUPS) for part in range(3)]
    operands = [qkvs[g] for g in range(N_GROUPS) for _ in range(3)]
    return pl.pallas_call(
        _attn_kernel,
        grid=(nbatch, N_HEADS),
        in_specs=in_specs,
        out_specs=pl.BlockSpec((None, SEQ, HEAD_DIM), lambda b, h: (b, 0, h)),
        out_shape=jax.ShapeDtypeStruct((nbatch, SEQ, D_MODEL), BF16),
        scratch_shapes=[
            pltpu.VMEM((3, QB, QB + 2 * HALF_W), F32),
            pltpu.VMEM((N_GROUPS, SEQ, HEAD_DIM), F32),
            pltpu.VMEM((N_GROUPS, SEQ, HEAD_DIM), F32),
            pltpu.VMEM((N_GROUPS, SEQ, HEAD_DIM), F32),
        ],
        compiler_params=_cparams(("arbitrary", "arbitrary")),
        name="dilated_attention",
    )(*operands)


def _proj_ln_kernel(x_ref, o_ref, w_ref, g_ref, b_ref, y_ref):
    h = jnp.dot(o_ref[...], w_ref[...], preferred_element_type=F32)
    y_ref[...] = _layer_norm(ALPHA * x_ref[...] + h, g_ref[...], b_ref[...])


def _proj_ln(x2, o2, w, g, b):
    t = x2.shape[0]
    row = pl.BlockSpec((TM, D_MODEL), lambda i: (i, 0))
    vec = pl.BlockSpec((1, D_MODEL), lambda i: (0, 0))
    return pl.pallas_call(
        _proj_ln_kernel,
        grid=(t // TM,),
        in_specs=[row, row, pl.BlockSpec((D_MODEL, D_MODEL), lambda i: (0, 0)), vec, vec],
        out_specs=row,
        out_shape=jax.ShapeDtypeStruct((t, D_MODEL), F32),
        compiler_params=_cparams(("arbitrary",)),
        name="attn_out_ln",
    )(x2, o2, w, g, b)


def _conv_kernel(xp_ref, x_ref, xn_ref, win_ref, ck_ref, wout_ref, g_ref, b_ref, y_ref, xcat, u_scr):
    i = pl.program_id(0)
    xcat[0:HALO, :] = xp_ref[0].astype(BF16)
    xcat[HALO:HALO + TM, :] = x_ref[...].astype(BF16)
    xcat[HALO + TM:, :] = xn_ref[0].astype(BF16)
    u_scr[...] = jnp.dot(xcat[...], win_ref[...], preferred_element_type=F32)
    hh = u_scr[:, D_MODEL:2 * D_MODEL] * u_scr[:, 2 * D_MODEL:]
    pos = (i * TM + lax.broadcasted_iota(jnp.int32, (TM, 1), 0)) % SEQ
    prev = jnp.where(pos == 0, 0.0, hh[HALO - 1:HALO - 1 + TM, :])
    nxt = jnp.where(pos == SEQ - 1, 0.0, hh[HALO + 1:HALO + 1 + TM, :])
    ck = ck_ref[...]
    conv = prev * ck[0:1, :] + hh[HALO:HALO + TM, :] * ck[1:2, :] + nxt * ck[2:3, :]
    gated = (u_scr[HALO:HALO + TM, 0:D_MODEL] * conv).astype(BF16)
    h = jnp.dot(gated, wout_ref[...], preferred_element_type=F32)
    y_ref[...] = _layer_norm(ALPHA * x_ref[...] + h, g_ref[...], b_ref[...])


def _conv_mixer_ln(x2, w_in, ck, w_out, g, b):
    t = x2.shape[0]
    nblk = t // HALO
    x3 = x2.reshape(nblk, HALO, D_MODEL)
    per = TM // HALO
    row = pl.BlockSpec((TM, D_MODEL), lambda i: (i, 0))
    vec = pl.BlockSpec((1, D_MODEL), lambda i: (0, 0))
    return pl.pallas_call(
        _conv_kernel,
        grid=(t // TM,),
        in_specs=[
            pl.BlockSpec((1, HALO, D_MODEL), lambda i: (jnp.maximum(i * per - 1, 0), 0, 0)),
            row,
            pl.BlockSpec((1, HALO, D_MODEL), lambda i: (jnp.minimum((i + 1) * per, nblk - 1), 0, 0)),
            pl.BlockSpec((D_MODEL, 3 * D_MODEL), lambda i: (0, 0)),
            pl.BlockSpec((3, D_MODEL), lambda i: (0, 0)),
            pl.BlockSpec((D_MODEL, D_MODEL), lambda i: (0, 0)),
            vec, vec,
        ],
        out_specs=row,
        out_shape=jax.ShapeDtypeStruct((t, D_MODEL), F32),
        scratch_shapes=[
            pltpu.VMEM((TM + 2 * HALO, D_MODEL), BF16),
            pltpu.VMEM((TM + 2 * HALO, 3 * D_MODEL), F32),
        ],
        compiler_params=_cparams(("arbitrary",)),
        name="conv_mixer_ln",
    )(x3, x2, x3, w_in, ck, w_out, g, b)


def _batcher_network(n):
    def merge(lo, hi, r):
        step = r * 2
        if step < hi - lo:
            yield from merge(lo, hi, step)
            yield from merge(lo + r, hi, step)
            yield from ((i, i + r) for i in range(lo + r, hi - r, step))
        else:
            yield (lo, lo + r)

    def sort(lo, hi):
        if hi - lo >= 1:
            mid = lo + (hi - lo) // 2
            yield from sort(lo, mid)
            yield from sort(mid + 1, hi)
            yield from merge(lo, hi, 1)

    return tuple(sort(0, n - 1))


_SORT16 = _batcher_network(PEER_TOPK)
_BITONIC16 = tuple((i, i + d) for d in (8, 4, 2, 1) for i in range(PEER_TOPK) if not i & d)
_CAND_ROW_LIMIT = tuple(PEER_TOPK // (a + 1) for a in range(PEER_TOPK))


def _compare_exchange(v, pairs):
    for i, j in pairs:
        v[i], v[j] = jnp.maximum(v[i], v[j]), jnp.minimum(v[i], v[j])


def _top16_sorted(rows):
    v = list(rows)
    _compare_exchange(v, _SORT16)
    for shift in (4, 2, 1):
        other = [pltpu.roll(x, shift, axis=0) for x in v]
        v = [jnp.maximum(v[k], other[PEER_TOPK - 1 - k]) for k in range(PEER_TOPK)]
        _compare_exchange(v, _BITONIC16)
    return v


def _dup16(x):
    hi = pltpu.bitcast(x, U32) & jnp.uint32(0xFFFF0000)
    return hi | (hi >> 16)


def _pack2(lo, hi):
    lo = pltpu.bitcast(lo.astype(BF16).astype(F32), U32)
    hi = pltpu.bitcast(hi.astype(BF16).astype(F32), U32)
    return (hi & jnp.uint32(0xFFFF0000)) | (lo >> 16)


def _select_block(s0, s1):
    nrow = len(s0)
    v0 = _top16_sorted(s0)
    v1 = _top16_sorted(s1)
    lanes = s0[0].shape[1]
    row = lax.broadcasted_iota(jnp.int32, (SUBLANES, lanes), 0)

    def rows_of(vals):
        out = vals[0]
        for r in range(1, SUBLANES):
            out = jnp.where(row == r, vals[r], out)
        return out

    v1_lo, v1_hi, v0_hi = rows_of(v1[:SUBLANES]), rows_of(v1[SUBLANES:]), rows_of(v0[SUBLANES:])
    cands = [v0[0] + v1_lo, v0[0] + v1_hi]
    for a in range(1, SUBLANES):
        cands.append(jnp.where(row < _CAND_ROW_LIMIT[a], v0[a] + v1_lo, -jnp.inf))
    cands.append(v0_hi + v1[0])
    work = list(cands)
    tau = None
    for k in range(PEER_TOPK):
        m = work[0]
        for w in work[1:]:
            m = jnp.maximum(m, w)
        tau = jnp.max(m, axis=0, keepdims=True)
        if k + 1 < PEER_TOPK:
            work = [jnp.where(w == tau, -jnp.inf, w) for w in work]
    top = v0[0] + v1[0]
    sel = [c >= tau for c in cands]
    z = jnp.zeros((1, lanes), F32)
    for c, m in zip(cands, sel):
        z = z + jnp.sum(jnp.where(m, jnp.exp(c - top), 0.0), axis=0, keepdims=True)
    ones = [jnp.where(m, 1.0, 0.0) for m in sel]
    counts = [jnp.sum(ones[0] + ones[1], axis=0, keepdims=True)]
    counts += [jnp.sum(ones[1 + a], axis=0, keepdims=True) for a in range(1, SUBLANES)]
    tau_b = jnp.broadcast_to(tau, (SUBLANES, lanes))
    cnt = [jnp.where(s0[k] < v0[SUBLANES - 1],
                     jnp.where(s0[k] >= v0[PEER_TOPK - 1], jnp.where(s0[k] + v1[0] >= tau_b, 1.0, 0.0), 0.0),
                     0.0) for k in range(nrow)]
    for a in range(SUBLANES):
        ca = jnp.broadcast_to(counts[a], (SUBLANES, lanes))
        cnt = [jnp.where(s0[k] == v0[a], ca, cnt[k]) for k in range(nrow)]
    zinv = jnp.broadcast_to(1.0 / z, (SUBLANES, lanes))
    a_fac = [jnp.exp(s0[k] - v0[0]) * zinv for k in range(nrow)]
    b_fac = [jnp.exp(s1[k] - v1[0]) for k in range(nrow)]
    rank1 = [jnp.zeros((SUBLANES, lanes), F32) for _ in range(nrow)]
    for b in range(PEER_TOPK):
        rank1 = [jnp.where(v1[b] > s1[k], float(b + 1), rank1[k]) for k in range(nrow)]
    return cnt, a_fac, rank1, b_fac


def _peer_select_kernel(x_ref, wq_ref, sk_ref, cp_ref, ap_ref, rp_ref, bp_ref):
    xb = x_ref[...].astype(BF16)
    nt = (((1,), (1,)), ((), ()))
    nrow = PEER_KEYS // SUBLANES
    half = nrow // 2
    for hd in range(SELECT_HEADS):
        qcols = slice(hd * 2 * PEER_HALF, (hd + 1) * 2 * PEER_HALF)
        q = jnp.dot(xb, wq_ref[:, qcols], preferred_element_type=F32).astype(BF16)
        s0 = lax.dot_general(sk_ref[hd, 0], q[:, :PEER_HALF], nt, preferred_element_type=F32)
        s1 = lax.dot_general(sk_ref[hd, 1], q[:, PEER_HALF:], nt, preferred_element_type=F32)
        for tb in range(TS // LANES):
            sl = slice(tb * LANES, (tb + 1) * LANES)
            r0 = [s0[SUBLANES * k:SUBLANES * (k + 1), sl] for k in range(nrow)]
            r1 = [s1[SUBLANES * k:SUBLANES * (k + 1), sl] for k in range(nrow)]
            cnt, a_fac, rank1, b_fac = _select_block(r0, r1)
            for k in range(nrow):
                rows = slice(SUBLANES * k, SUBLANES * (k + 1))
                cp_ref[hd, rows, sl] = _dup16(cnt[k])
                ap_ref[hd, rows, sl] = _dup16(a_fac[k].astype(BF16).astype(F32))
            for k in range(half):
                rows = slice(SUBLANES * k, SUBLANES * (k + 1))
                rp_ref[hd, rows, sl] = _pack2(rank1[k], rank1[k + half])
                bp_ref[hd, rows, sl] = _pack2(b_fac[k], b_fac[k + half])


def _peer_select(x2, wq, sk):
    t = x2.shape[0]
    full = jax.ShapeDtypeStruct((PEER_HEADS, PEER_KEYS, t), U32)
    packed = jax.ShapeDtypeStruct((PEER_HEADS, PEER_KEYS // 2, t), U32)
    fspec = pl.BlockSpec((SELECT_HEADS, PEER_KEYS, TS), lambda i, h: (h, 0, i))
    pspec = pl.BlockSpec((SELECT_HEADS, PEER_KEYS // 2, TS), lambda i, h: (h, 0, i))
    return pl.pallas_call(
        _peer_select_kernel,
        grid=(t // TS, PEER_HEADS // SELECT_HEADS),
        in_specs=[
            pl.BlockSpec((TS, D_MODEL), lambda i, h: (i, 0)),
            pl.BlockSpec((D_MODEL, SELECT_HEADS * 2 * PEER_HALF), lambda i, h: (0, h)),
            pl.BlockSpec((SELECT_HEADS, 2, PEER_KEYS, PEER_HALF), lambda i, h: (h, 0, 0, 0)),
        ],
        out_specs=[fspec, fspec, pspec, pspec],
        out_shape=[full, full, packed, packed],
        compiler_params=_cparams(("arbitrary", "arbitrary")),
        name="peer_select",
    )(x2, wq, sk)


def _peer_dense_kernel(x_ref, cp_ref, ap_ref, rp_ref, bp_ref, u_ref, vt_ref, g_ref, bb_ref,
                       p_ref, wg_ref, wp_ref, y_ref, *scratch):
    e = pl.program_id(1)
    half = PEER_KEYS // 2
    xt_scr, acc_scr, c_first, c_second = scratch[:4]
    h_scrs = scratch[4:]
    rows_per_chunk = SUBLANES // DENSE_CHUNKS
    ec = rows_per_chunk * PEER_KEYS
    eh = ET // 2

    @pl.when(e == 0)
    def _():
        xt_scr[...] = x_ref[...].T.astype(BF16)
        acc_scr[...] = jnp.zeros(acc_scr.shape, F32)

    irows = pl.ds(pl.multiple_of(e * SUBLANES, SUBLANES), SUBLANES)

    def gelu(hv):
        return 0.5 * hv * (1.0 + lax.erf(hv * INV_SQRT2))

    def scores(ch):
        h_scrs[ch][...] = jnp.dot(u_ref[ch * ec:(ch + 1) * ec, :], xt_scr[...], preferred_element_type=F32)

    pieces = DENSE_CHUNKS // 2

    def coefficients(ch, c_scr):
        h_scr = h_scrs[ch]
        for tb in range(TT // LANES):
            sl = slice(tb * LANES, (tb + 1) * LANES)
            ws = [jnp.zeros((PEER_KEYS, LANES), BF16) for _ in range(rows_per_chunk)]
            for hd in range(PEER_HEADS):
                rank1 = pltpu.bitcast(rp_ref[hd, :, sl], BF16)
                b_fac = pltpu.bitcast(bp_ref[hd, :, sl], BF16)
                cnt8 = cp_ref[hd, irows, sl]
                a8 = ap_ref[hd, irows, sl]
                for ir in range(rows_per_chunk):
                    il = ch * rows_per_chunk + ir
                    cnt_row = jnp.broadcast_to(cnt8[il:il + 1, :], (half, LANES))
                    a_row = jnp.broadcast_to(a8[il:il + 1, :], (half, LANES))
                    picked = jnp.where(rank1 < pltpu.bitcast(cnt_row, BF16), b_fac, jnp.zeros_like(b_fac))
                    ws[ir] = ws[ir] + picked * pltpu.bitcast(a_row, BF16)
            for ir in range(rows_per_chunk):
                il = ch * rows_per_chunk + ir
                wu = pltpu.bitcast(ws[ir], U32)
                w_lo = pltpu.bitcast(wu << 16, F32)
                w_hi = pltpu.bitcast(wu & jnp.uint32(0xFFFF0000), F32)
                r0 = (il * PEER_KEYS) % eh
                h0 = ir * PEER_KEYS
                c_scr[r0:r0 + half, sl] = (w_lo * gelu(h_scr[h0:h0 + half, sl])).astype(BF16)
                c_scr[r0 + half:r0 + PEER_KEYS, sl] = (
                    w_hi * gelu(h_scr[h0 + half:h0 + PEER_KEYS, sl])).astype(BF16)

    scores(0)
    for ch in range(DENSE_CHUNKS):
        if ch + 1 < DENSE_CHUNKS:
            scores(ch + 1)
        coefficients(ch, c_first if ch < pieces else c_second)
    acc_scr[...] += jnp.dot(vt_ref[:, :eh], c_first[...], preferred_element_type=F32)
    acc_scr[...] += jnp.dot(vt_ref[:, eh:], c_second[...], preferred_element_type=F32)

    @pl.when(e == pl.num_programs(1) - 1)
    def _():
        f = acc_scr[...].T
        y = _layer_norm(ALPHA * x_ref[...] + f, g_ref[...], bb_ref[...])
        gate = jax.nn.sigmoid(jnp.dot(y.astype(BF16), wg_ref[...], preferred_element_type=F32))
        proj = jnp.dot(p_ref[...].astype(BF16), wp_ref[...], preferred_element_type=F32)
        y_ref[...] = y + gate * proj


def _peer_dense(x2, sel, u_all, vt_all, g, b, p_all, wg, wp, layer):
    t = x2.shape[0]
    row = pl.BlockSpec((TT, D_MODEL), lambda i, e: (i, 0))
    vec = pl.BlockSpec((1, D_MODEL), lambda i, e: (0, 0))
    full = pl.BlockSpec((PEER_HEADS, PEER_KEYS, TT), lambda i, e: (0, 0, i))
    packed = pl.BlockSpec((PEER_HEADS, PEER_KEYS // 2, TT), lambda i, e: (0, 0, i))
    return pl.pallas_call(
        _peer_dense_kernel,
        grid=(t // TT, PEER_EXPERTS // ET),
        in_specs=[
            row, full, full, packed, packed,
            pl.BlockSpec((None, ET, D_MODEL), lambda i, e: (layer, e, 0)),
            pl.BlockSpec((None, D_MODEL, ET), lambda i, e: (layer, 0, e)),
            vec, vec,
            pl.BlockSpec((None, TT, PLE_DIM), lambda i, e: (layer, i, 0)),
            pl.BlockSpec((D_MODEL, D_MODEL), lambda i, e: (0, 0)),
            pl.BlockSpec((PLE_DIM, D_MODEL), lambda i, e: (0, 0)),
        ],
        out_specs=row,
        out_shape=jax.ShapeDtypeStruct((t, D_MODEL), F32),
        scratch_shapes=(
            [pltpu.VMEM((D_MODEL, TT), BF16),
             pltpu.VMEM((D_MODEL, TT), F32),
             pltpu.VMEM((ET // 2, TT), BF16),
             pltpu.VMEM((ET // 2, TT), BF16)]
            + [pltpu.VMEM((ET // DENSE_CHUNKS, TT), F32)] * DENSE_CHUNKS
        ),
        compiler_params=_cparams(("arbitrary", "arbitrary")),
        name="peer_dense",
    )(x2, *sel, u_all, vt_all, g, b, p_all, wg, wp)


def _rope_tables(dil):
    inv_freq = ROPE_THETA ** (-jnp.arange(0, ROT_DIM, 2, dtype=F32) / ROT_DIM)
    ang = jnp.arange(SEQ, dtype=F32)[:, None] * inv_freq[None, :]
    cos, sin = jnp.cos(ang), jnp.sin(ang)
    half = ROT_DIM // 2
    pad = HEAD_DIM - ROT_DIM
    c_tab = jnp.concatenate([cos, cos, jnp.ones((SEQ, pad), F32)], axis=1)
    sa_tab = jnp.concatenate([-sin, jnp.zeros((SEQ, half + pad), F32)], axis=1)
    sb_tab = jnp.concatenate([jnp.zeros((SEQ, half), F32), sin, jnp.zeros((SEQ, pad), F32)], axis=1)

    def residue_major(tab):
        return tab.reshape(SEQ // dil, dil, HEAD_DIM).transpose(1, 0, 2).reshape(SEQ, HEAD_DIM)

    return tuple(residue_major(tab) for tab in (c_tab, sa_tab, sb_tab))


def _trunk(x, p, attn_w_qkv, attn_w_out, conv_w_in, conv_kernel, conv_w_out,
           peer_w_query, peer_sub_keys, peer_u, peer_vt,
           ln1_gain, ln1_bias, ln2_gain, ln2_bias, ple_w_proj, ple_w_gate):
    nbatch = x.shape[0]
    t = nbatch * SEQ
    x2 = x.reshape(t, D_MODEL)
    tabs = [_rope_tables(dil) for dil in DILATIONS]
    p_all = p.reshape(DEPTH, t, PLE_DIM)
    for i in range(DEPTH):
        j = i // N_MIXERS
        g1 = ln1_gain[i].reshape(1, D_MODEL)
        b1 = ln1_bias[i].reshape(1, D_MODEL)
        if i % N_MIXERS == 0:
            x3 = x2.reshape(nbatch, SEQ, D_MODEL)
            qkvs = [_qkv_proj(x3, attn_w_qkv, j, g, tabs[g], dil) for g, dil in enumerate(DILATIONS)]
            o = _attention(qkvs)
            x2 = _proj_ln(x2, o.reshape(t, D_MODEL), attn_w_out[j], g1, b1)
        else:
            x2 = _conv_mixer_ln(x2, conv_w_in[j], conv_kernel[j], conv_w_out[j], g1, b1)
        sel = _peer_select(x2, peer_w_query[i], peer_sub_keys[i])
        x2 = _peer_dense(
            x2, sel, peer_u, peer_vt,
            ln2_gain[i].reshape(1, D_MODEL), ln2_bias[i].reshape(1, D_MODEL),
            p_all, ple_w_gate[i], ple_w_proj[i], i)
    return x2.reshape(nbatch, SEQ, D_MODEL)


def kernel(x_prompt, x_sample, p_prompt, p_sample, attn_w_qkv, attn_w_out, conv_w_in, conv_kernel, conv_w_out, peer_w_query, peer_sub_keys, peer_u, peer_v, ln1_gain, ln1_bias, ln2_gain, ln2_bias, ple_w_proj, ple_w_gate):
    weights = (attn_w_qkv.astype(BF16), attn_w_out.astype(BF16), conv_w_in.astype(BF16), conv_kernel,
               conv_w_out.astype(BF16), peer_w_query.astype(BF16), peer_sub_keys.astype(BF16),
               peer_u.astype(BF16), peer_v.astype(BF16).transpose(0, 2, 1),
               ln1_gain, ln1_bias, ln2_gain, ln2_bias, ple_w_proj.astype(BF16), ple_w_gate.astype(BF16))
    return (_trunk(x_prompt, p_prompt, *weights), _trunk(x_sample, p_sample, *weights))
```
